```python
import math
import jax, jax.numpy as jnp
from jax import lax
import numpy as np

D_MODEL = 2048
BATCH = 4
SEQ = 4096
DEPTH = 1

N_META = 16
MIX_WIDTH = D_MODEL
SSM_WIDTH = MIX_WIDTH // 2
ATTN_WIDTH = MIX_WIDTH - SSM_WIDTH
SSM_GROUP_CH = 16
SSM_GROUPS = SSM_WIDTH // SSM_GROUP_CH
SSM_STATE = 64
ATTN_HEAD_DIM = 64
ATTN_HEADS = ATTN_WIDTH // ATTN_HEAD_DIM
Q_BLOCK = 128
D_FF = (11 * D_MODEL) // 4
CONV_W = 3
IN_PROJ = SSM_WIDTH + 3 * ATTN_WIDTH
RMS_EPS = 1e-6
DT_MIN = 1e-3
DT_MAX = 1e-1

kernel_name = "hymba_s5_stickbreak_convffn"


def rmsnorm(x, g):
    xf = x.astype(jnp.float32)
    y = xf * lax.rsqrt(jnp.mean(xf * xf, axis=-1, keepdims=True) + RMS_EPS)
    return (y * g.astype(jnp.float32)).astype(x.dtype)


def _complex_scan_combine(e1, e2):
    a1r, a1i, b1r, b1i = e1
    a2r, a2i, b2r, b2i = e2
    ar = a2r * a1r - a2i * a1i
    ai = a2r * a1i + a2i * a1r
    br = a2r * b1r - a2i * b1i + b2r
    bi = a2r * b1i + a2i * b1r + b2i
    return (ar, ai, br, bi)


def s5_mixer(u, lam_re, lam_im, log_dt, b_re, b_im, c_re, c_im, d_skip, w_glu, b_glu):
    f32 = jnp.float32
    bsz, L, _ = u.shape
    uf = u.astype(f32).reshape(bsz, L, SSM_GROUPS, SSM_GROUP_CH)
    lr = jnp.minimum(lam_re.astype(f32), -1e-4)
    li = lam_im.astype(f32)
    delta = jnp.exp(log_dt.astype(f32))[:, None]
    mag = jnp.exp(lr * delta)
    ar = mag * jnp.cos(li * delta)
    ai = mag * jnp.sin(li * delta)
    den = lr * lr + li * li
    nr = ar - 1.0
    ni = ai
    fr = (nr * lr + ni * li) / den
    fi = (ni * lr - nr * li) / den
    brf = b_re.astype(f32)
    bif = b_im.astype(f32)
    bbar_r = fr[..., None] * brf - fi[..., None] * bif
    bbar_i = fr[..., None] * bif + fi[..., None] * brf
    bu_r = jnp.einsum('blgh,gph->blgp', uf, bbar_r)
    bu_i = jnp.einsum('blgh,gph->blgp', uf, bbar_i)
    a_r = jnp.broadcast_to(ar[None, None], (1, L, SSM_GROUPS, SSM_STATE))
    a_i = jnp.broadcast_to(ai[None, None], (1, L, SSM_GROUPS, SSM_STATE))
    _, _, h_r, h_i = lax.associative_scan(_complex_scan_combine, (a_r, a_i, bu_r, bu_i), axis=1)
    y = (jnp.einsum('blgp,ghp->blgh', h_r, c_re.astype(f32))
         - jnp.einsum('blgp,ghp->blgh', h_i, c_im.astype(f32))
         + d_skip.astype(f32) * uf)
    y = jax.nn.gelu(y.reshape(bsz, L, SSM_WIDTH))
    y = y * jax.nn.sigmoid(y @ w_glu.astype(f32) + b_glu.astype(f32))
    return y.astype(u.dtype)


def _stick_breaking_block(q_blk, qpos, k, v, kpos):
    scale = 1.0 / math.sqrt(ATTN_HEAD_DIM)
    z = jnp.einsum('bqhd,bkhd->bhqk', q_blk, k).astype(jnp.float32) * scale
    valid = (kpos[None, :] < qpos[:, None])[None, None]
    log_keep = jnp.where(valid, jax.nn.log_sigmoid(-z), 0.0)
    suffix = lax.cumsum(log_keep, axis=3, reverse=True) - log_keep
    w = jnp.where(valid, jnp.exp(jax.nn.log_sigmoid(z) + suffix), 0.0)
    return jnp.einsum('bhqk,bkhd->bqhd', w.astype(v.dtype), v)


def stick_breaking_attention(q, k, v):
    bsz, L, H, Dh = q.shape
    kpos = jnp.arange(L, dtype=jnp.int32)
    out_meta = _stick_breaking_block(q[:, :N_META], jnp.arange(N_META, dtype=jnp.int32), k, v, kpos)
    n_real = L - N_META
    nb = n_real // Q_BLOCK
    qb = q[:, N_META:].reshape(bsz, nb, Q_BLOCK, H, Dh).transpose(1, 0, 2, 3, 4)
    qpos = (N_META + jnp.arange(n_real, dtype=jnp.int32)).reshape(nb, Q_BLOCK)
    out = lax.map(lambda a: _stick_breaking_block(a[0], a[1], k, v, kpos), (qb, qpos))
    out = out.transpose(1, 0, 2, 3, 4).reshape(bsz, n_real, H, Dh)
    return jnp.concatenate([out_meta, out], axis=1)


def causal_depthwise_conv(x, w, b):
    c = x.shape[-1]
    y = lax.conv_general_dilated(x, w[:, None, :], window_strides=(1,), padding=[(CONV_W - 1, 0)],
                                 dimension_numbers=('NWC', 'WIO', 'NWC'), feature_group_count=c)
    return y + b


def setup_inputs(seed: int = 0) -> dict:
    key = jax.random.key(seed)
    ks = jax.random.split(key, 24)
    f32 = jnp.float32
    n = lambda k, s, sc: jax.random.normal(k, s, f32) * sc
    x = n(ks[0], (BATCH, SEQ, D_MODEL), 1.0)
    meta_tokens = n(ks[1], (N_META, D_MODEL), 1.0)
    norm_mix_g = 1.0 + n(ks[2], (DEPTH, D_MODEL), 0.02)
    w_in = n(ks[3], (DEPTH, D_MODEL, IN_PROJ), D_MODEL ** -0.5)
    ssm_lambda_re = -0.5 + n(ks[4], (DEPTH, SSM_GROUPS, SSM_STATE), 0.01)
    ssm_lambda_im = jnp.broadcast_to(jnp.pi * jnp.arange(SSM_STATE, dtype=f32),
                                     (DEPTH, SSM_GROUPS, SSM_STATE)) + 0.0
    ssm_log_dt = jax.random.uniform(ks[5], (DEPTH, SSM_GROUPS), f32,
                                    math.log(DT_MIN), math.log(DT_MAX))
    b_scale = (2.0 * SSM_GROUP_CH) ** -0.5
    ssm_b_re = n(ks[6], (DEPTH, SSM_GROUPS, SSM_STATE, SSM_GROUP_CH), b_scale)
    ssm_b_im = n(ks[7], (DEPTH, SSM_GROUPS, SSM_STATE, SSM_GROUP_CH), b_scale)
    c_scale = (2.0 * SSM_STATE) ** -0.5
    ssm_c_re = n(ks[8], (DEPTH, SSM_GROUPS, SSM_GROUP_CH, SSM_STATE), c_scale)
    ssm_c_im = n(ks[9], (DEPTH, SSM_GROUPS, SSM_GROUP_CH, SSM_STATE), c_scale)
    ssm_d = n(ks[10], (DEPTH, SSM_GROUPS, SSM_GROUP_CH), 1.0)
    w_glu = n(ks[11], (DEPTH, SSM_WIDTH, SSM_WIDTH), SSM_WIDTH ** -0.5)
    b_glu = n(ks[12], (DEPTH, SSM_WIDTH), 0.01)
    g_ssm_out = 1.0 + n(ks[13], (DEPTH, SSM_WIDTH), 0.02)
    g_attn_out = 1.0 + n(ks[14], (DEPTH, ATTN_WIDTH), 0.02)
    w_out = n(ks[15], (DEPTH, MIX_WIDTH, D_MODEL), MIX_WIDTH ** -0.5)
    norm_ffn_g = 1.0 + n(ks[16], (DEPTH, D_MODEL), 0.02)
    w_up = n(ks[17], (DEPTH, D_MODEL, 2 * D_FF), D_MODEL ** -0.5)
    conv_w = n(ks[18], (DEPTH, CONV_W, 2 * D_FF), CONV_W ** -0.5)
    conv_b = n(ks[19], (DEPTH, 2 * D_FF), 0.01)
    w_down = n(ks[20], (DEPTH, D_FF, D_MODEL), D_FF ** -0.5)
    norm_final_g = 1.0 + n(ks[21], (D_MODEL,), 0.02)
    return {"x": x, "meta_tokens": meta_tokens, "norm_mix_g": norm_mix_g, "w_in": w_in,
            "ssm_lambda_re": ssm_lambda_re, "ssm_lambda_im": ssm_lambda_im, "ssm_log_dt": ssm_log_dt,
            "ssm_b_re": ssm_b_re, "ssm_b_im": ssm_b_im, "ssm_c_re": ssm_c_re, "ssm_c_im": ssm_c_im,
            "ssm_d": ssm_d, "w_glu": w_glu, "b_glu": b_glu, "g_ssm_out": g_ssm_out,
            "g_attn_out": g_attn_out, "w_out": w_out, "norm_ffn_g": norm_ffn_g, "w_up": w_up,
            "conv_w": conv_w, "conv_b": conv_b, "w_down": w_down, "norm_final_g": norm_final_g}


def reference(x, meta_tokens, norm_mix_g, w_in, ssm_lambda_re, ssm_lambda_im, ssm_log_dt,
              ssm_b_re, ssm_b_im, ssm_c_re, ssm_c_im, ssm_d, w_glu, b_glu, g_ssm_out,
              g_attn_out, w_out, norm_ffn_g, w_up, conv_w, conv_b, w_down, norm_final_g):
    bsz = x.shape[0]
    meta = jnp.broadcast_to(meta_tokens[None].astype(x.dtype), (bsz, N_META, D_MODEL))
    h = jnp.concatenate([meta, x], axis=1)
    L = h.shape[1]
    for i in range(DEPTH):
        hn = rmsnorm(h, norm_mix_g[i])
        proj = hn @ w_in[i]
        u = proj[..., :SSM_WIDTH]
        q, k, v = jnp.split(proj[..., SSM_WIDTH:], 3, axis=-1)
        q = q.reshape(bsz, L, ATTN_HEADS, ATTN_HEAD_DIM)
        k = k.reshape(bsz, L, ATTN_HEADS, ATTN_HEAD_DIM)
        v = v.reshape(bsz, L, ATTN_HEADS, ATTN_HEAD_DIM)
        y_ssm = s5_mixer(u, ssm_lambda_re[i], ssm_lambda_im[i], ssm_log_dt[i], ssm_b_re[i],
                         ssm_b_im[i], ssm_c_re[i], ssm_c_im[i], ssm_d[i], w_glu[i], b_glu[i])
        y_attn = stick_breaking_attention(q, k, v).reshape(bsz, L, ATTN_WIDTH)
        mixed = jnp.concatenate([rmsnorm(y_ssm, g_ssm_out[i]), rmsnorm(y_attn, g_attn_out[i])], axis=-1)
        h = h + mixed @ w_out[i]
        hn = rmsnorm(h, norm_ffn_g[i])
        up = causal_depthwise_conv(hn @ w_up[i], conv_w[i], conv_b[i])
        gate, val = jnp.split(up, 2, axis=-1)
        h = h + (jax.nn.silu(gate) * val) @ w_down[i]
    out = rmsnorm(h, norm_final_g)
    return out[:, N_META:]
```

```python
import functools

import jax
import jax.numpy as jnp
from jax import lax
from jax.experimental import pallas as pl
from jax.experimental.pallas import tpu as pltpu

F32 = jnp.float32
BF16 = jnp.bfloat16

N_META = 16
SSM_WIDTH = 1024
ATTN_WIDTH = 1024
SSM_GROUP_CH = 16
SSM_GROUPS = 64
SSM_STATE = 64
HEAD_DIM = 64
CONV_W = 3
RMS_EPS = 1e-6

LANES = 128
GROUPS_PER_BLOCK = LANES // SSM_GROUP_CH
N_SSM_BLOCKS = SSM_GROUPS // GROUPS_PER_BLOCK
STATE_LANES = GROUPS_PER_BLOCK * SSM_STATE
HALO = 16
VMEM_LIMIT = 56 * 1024 * 1024


def _rms(x, g):
    ms = jnp.mean(x * x, axis=-1, keepdims=True)
    return x * lax.rsqrt(ms + RMS_EPS) * g


def _sigmoid(x):
    return 1.0 / (1.0 + jnp.exp(-x))


def _norm_inproj_kernel(x_ref, g_ref, w_ref, o_ref, xn_ref):
    @pl.when(pl.program_id(1) == 0)
    def _():
        xn_ref[...] = _rms(x_ref[...], g_ref[...]).astype(BF16)

    o_ref[...] = jnp.dot(xn_ref[...], w_ref[...], preferred_element_type=F32).astype(o_ref.dtype)


def _norm_inproj(x, g, w):
    m, d = x.shape
    n = w.shape[1]
    tm = min(512, m)
    tn = 1024
    return pl.pallas_call(
        _norm_inproj_kernel,
        grid=(m // tm, n // tn),
        in_specs=[
            pl.BlockSpec((tm, d), lambda i, j: (i, 0)),
            pl.BlockSpec((1, d), lambda i, j: (0, 0)),
            pl.BlockSpec((d, tn), lambda i, j: (0, j)),
        ],
        out_specs=pl.BlockSpec((tm, tn), lambda i, j: (i, j)),
        out_shape=jax.ShapeDtypeStruct((m, n), BF16),
        scratch_shapes=[pltpu.VMEM((tm, d), BF16)],
        compiler_params=pltpu.CompilerParams(
            dimension_semantics=("parallel", "arbitrary"), vmem_limit_bytes=VMEM_LIMIT),
        name="norm_inproj",
    )(x, g, w)


def _ssm_discretize_kernel(lr_ref, li_ref, ldt_ref, bre_ref, bim_ref,
                           ar_ref, ai_ref, bbr_ref, bbi_ref):
    lr = jnp.minimum(lr_ref[...], -1e-4)
    li = li_ref[...]
    delta = jnp.exp(ldt_ref[...])
    mag = jnp.exp(lr * delta)
    ar = mag * jnp.cos(li * delta)
    ai = mag * jnp.sin(li * delta)
    den = lr * lr + li * li
    nr = ar - 1.0
    ni = ai
    fr = (nr * lr + ni * li) / den
    fi = (ni * lr - nr * li) / den
    bre = bre_ref[...]
    bim = bim_ref[...]
    ar_ref[...] = ar
    ai_ref[...] = ai
    bbr_ref[...] = fr * bre - fi * bim
    bbi_ref[...] = fr * bim + fi * bre


def _ssm_discretize(lam_re, lam_im, log_dt, b_re, b_im):
    g, p, h = b_re.shape
    rep = lambda a: jnp.repeat(a, h, axis=1)
    shape = jax.ShapeDtypeStruct((g, p * h), F32)
    ar, ai, bbr, bbi = pl.pallas_call(
        _ssm_discretize_kernel,
        out_shape=(shape, shape, shape, shape),
        name="ssm_discretize",
    )(rep(lam_re), rep(lam_im), jnp.broadcast_to(log_dt[:, None], (g, p * h)),
      b_re.reshape(g, p * h), b_im.reshape(g, p * h))
    return ar[:, ::h], ai[:, ::h], bbr.reshape(g, p, h), bbi.reshape(g, p, h)


def _block_diag(blocks):
    nb, gl, r, c = blocks.shape
    eye = jnp.eye(gl, dtype=blocks.dtype)
    return (blocks[:, :, :, None, :] * eye[None, :, None, :, None]).reshape(nb, gl * r, gl * c)


def _ssm_scan_kernel(u_ref, bb_ref, cr_ref, ci_ref, ar_ref, ai_ref, d_ref, h0_ref,
                     y_ref, hf_ref, bu_ref, hh_ref, h_ref, *, nb, tk):
    c = pl.program_id(1)
    sl = STATE_LANES

    @pl.when(c == 0)
    def _():
        h_ref[...] = jnp.broadcast_to(h0_ref[0], (nb, 2 * sl))

    u = u_ref[...].reshape(nb * tk, LANES)
    bu = jnp.dot(u, bb_ref[0], preferred_element_type=F32)
    nlb = sl // LANES
    for j in range(2 * nlb):
        bu_ref[j] = bu[:, j * LANES:(j + 1) * LANES]

    blk = lambda ref, j: jnp.broadcast_to(ref[0, :, j * LANES:(j + 1) * LANES], (nb, LANES))
    ar = [blk(ar_ref, j) for j in range(nlb)]
    ai = [blk(ai_ref, j) for j in range(nlb)]
    hr = [h_ref[:, j * LANES:(j + 1) * LANES] for j in range(nlb)]
    hi = [h_ref[:, sl + j * LANES:sl + (j + 1) * LANES] for j in range(nlb)]
    for k in range(tk):
        rows = pl.ds(k, nb, stride=tk) if nb > 1 else pl.ds(k, 1)
        for j in range(nlb):
            br = bu_ref[j, rows, :]
            bi = bu_ref[nlb + j, rows, :]
            hr[j], hi[j] = ar[j] * hr[j] - ai[j] * hi[j] + br, ar[j] * hi[j] + ai[j] * hr[j] + bi
            hh_ref[j, rows, :] = hr[j]
            hh_ref[nlb + j, rows, :] = hi[j]
    for j in range(nlb):
        h_ref[:, j * LANES:(j + 1) * LANES] = hr[j]
        h_ref[:, sl + j * LANES:sl + (j + 1) * LANES] = hi[j]

    y = d_ref[0] * u.astype(F32)
    for j in range(nlb):
        y = y + jnp.dot(hh_ref[j].astype(BF16), cr_ref[0, j * LANES:(j + 1) * LANES, :],
                        preferred_element_type=F32)
        y = y - jnp.dot(hh_ref[nlb + j].astype(BF16), ci_ref[0, j * LANES:(j + 1) * LANES, :],
                        preferred_element_type=F32)
    y_ref[...] = y.reshape(nb, tk, LANES).astype(y_ref.dtype)

    @pl.when(c == pl.num_programs(1) - 1)
    def _():
        hf_ref[0] = h_ref[...]


def _ssm_scan(proj3, h0, bblk, crblk, ciblk, ar, ai, dsk):
    nb, l, _ = proj3.shape
    tk = min(128, l)
    sl = STATE_LANES
    kern = functools.partial(_ssm_scan_kernel, nb=nb, tk=tk)
    return pl.pallas_call(
        kern,
        grid=(N_SSM_BLOCKS, l // tk),
        in_specs=[
            pl.BlockSpec((nb, tk, LANES), lambda s, c: (0, c, s)),
            pl.BlockSpec((1, LANES, 2 * sl), lambda s, c: (s, 0, 0)),
            pl.BlockSpec((1, sl, LANES), lambda s, c: (s, 0, 0)),
            pl.BlockSpec((1, sl, LANES), lambda s, c: (s, 0, 0)),
            pl.BlockSpec((1, 1, sl), lambda s, c: (s, 0, 0)),
            pl.BlockSpec((1, 1, sl), lambda s, c: (s, 0, 0)),
            pl.BlockSpec((1, 1, LANES), lambda s, c: (s, 0, 0)),
            pl.BlockSpec((1, 1, 2 * sl), lambda s, c: (s, 0, 0)),
        ],
        out_specs=[
            pl.BlockSpec((nb, tk, LANES), lambda s, c: (0, c, s)),
            pl.BlockSpec((1, nb, 2 * sl), lambda s, c: (s, 0, 0)),
        ],
        out_shape=[
            jax.ShapeDtypeStruct((nb, l, SSM_WIDTH), BF16),
            jax.ShapeDtypeStruct((N_SSM_BLOCKS, nb, 2 * sl), F32),
        ],
        scratch_shapes=[
            pltpu.VMEM((2 * sl // LANES, nb * tk, LANES), F32),
            pltpu.VMEM((2 * sl // LANES, nb * tk, LANES), F32),
            pltpu.VMEM((nb, 2 * sl), F32),
        ],
        compiler_params=pltpu.CompilerParams(
            dimension_semantics=("parallel", "arbitrary"), vmem_limit_bytes=VMEM_LIMIT),
        name="ssm_scan",
    )(proj3, bblk, crblk, ciblk, ar, ai, dsk, h0)


def _sb_tile(qm, k, v, uu, c, acc, *, causal, kmin):
    tq, tk = qm.shape[0], k.shape[0]
    z = lax.dot_general(qm, k, (((1,), (1,)), ((), ())), preferred_element_type=F32)
    l1p = jnp.log(1.0 + jnp.exp(-jnp.abs(z)))
    ls = jnp.minimum(z, 0.0) - l1p
    lk = ls - z
    valid = None
    if causal or kmin > 0:
        row = lax.broadcasted_iota(jnp.int32, (tq, tk), 0)
        col = lax.broadcasted_iota(jnp.int32, (tq, tk), 1)
        valid = col < row if causal else None
        if kmin > 0:
            vk = col >= kmin
            valid = vk if valid is None else valid & vk
        lk = jnp.where(valid, lk, 0.0)
    hi = lk.astype(BF16)
    lo = (lk - hi.astype(F32)).astype(BF16)
    r = (jnp.dot(hi, uu, preferred_element_type=F32) + jnp.dot(lo, uu, preferred_element_type=F32))
    w = jnp.exp(ls + r[:, :tk] + c)
    if valid is not None:
        w = jnp.where(valid, w, 0.0)
    acc = acc + jnp.dot(w.astype(BF16), v, preferred_element_type=F32)
    return c + r[:, tk:], acc


def _attn_kernel(q_ref, k_ref, v_ref, km_ref, vm_ref, uu_ref, o_ref, *, tq, has_prefix, kmin):
    qi = pl.program_id(2)
    q = q_ref[0] * 0.125
    uu = uu_ref[...]
    lane = lax.broadcasted_iota(jnp.int32, (tq, LANES), 1)
    outs = []
    for parity in range(2):
        qm = jnp.where((lane >= HEAD_DIM) == bool(parity), q, jnp.zeros_like(q))
        c = jnp.zeros((tq, LANES), F32)
        acc = jnp.zeros((tq, LANES), F32)
        d0 = pl.multiple_of(qi * tq, tq)
        c, acc = _sb_tile(qm, k_ref[0, pl.ds(d0, tq), :], v_ref[0, pl.ds(d0, tq), :], uu, c, acc,
                          causal=True, kmin=kmin)

        def body(i, carry):
            cc, aa = carry
            s0 = pl.multiple_of((qi - 1 - i) * tq, tq)
            return _sb_tile(qm, k_ref[0, pl.ds(s0, tq), :], v_ref[0, pl.ds(s0, tq), :], uu, cc, aa,
                            causal=False, kmin=0)

        c, acc = lax.fori_loop(0, qi, body, (c, acc))
        if has_prefix:
            c, acc = _sb_tile(qm, km_ref[...], vm_ref[...], uu, c, acc,
                              causal=False, kmin=LANES - N_META)
        outs.append(acc)
    o_ref[0] = jnp.where(lane < HEAD_DIM, outs[0], outs[1]).astype(o_ref.dtype)


def _attention(proj3, kmeta, vmeta, uu, *, has_prefix, kmin):
    nb, l, _ = proj3.shape
    tq = LANES
    nhp = ATTN_WIDTH // LANES
    qoff, koff, voff = SSM_WIDTH // LANES, (SSM_WIDTH + ATTN_WIDTH) // LANES, (SSM_WIDTH + 2 * ATTN_WIDTH) // LANES
    kern = functools.partial(_attn_kernel, tq=tq, has_prefix=has_prefix, kmin=kmin)
    return pl.pallas_call(
        kern,
        grid=(nb, nhp, l // tq),
        in_specs=[
            pl.BlockSpec((1, tq, LANES), lambda b, h, i: (b, i, qoff + h)),
            pl.BlockSpec((1, l, LANES), lambda b, h, i: (b, 0, koff + h)),
            pl.BlockSpec((1, l, LANES), lambda b, h, i: (b, 0, voff + h)),
            pl.BlockSpec((LANES, LANES), lambda b, h, i: (0, h)),
            pl.BlockSpec((LANES, LANES), lambda b, h, i: (0, h)),
            pl.BlockSpec((LANES, 2 * LANES), lambda b, h, i: (0, 0)),
        ],
        out_specs=pl.BlockSpec((1, tq, LANES), lambda b, h, i: (b, i, h)),
        out_shape=jax.ShapeDtypeStruct((nb, l, ATTN_WIDTH), BF16),
        compiler_params=pltpu.CompilerParams(
            dimension_semantics=("parallel", "parallel", "arbitrary"), vmem_limit_bytes=VMEM_LIMIT),
        name="sb_attention",
    )(proj3, proj3, proj3, kmeta, vmeta, uu)


def _mix_out_kernel(ys_ref, ya_ref, x_ref, wglu_ref, bglu_ref, gs_ref, ga_ref, wout_ref, gf_ref,
                    h1_ref, hn_ref):
    y = jax.nn.gelu(ys_ref[...].astype(F32))
    gate = jnp.dot(y.astype(BF16), wglu_ref[...], preferred_element_type=F32) + bglu_ref[...]
    y = y * _sigmoid(gate)
    n1 = _rms(y, gs_ref[...]).astype(BF16)
    n2 = _rms(ya_ref[...].astype(F32), ga_ref[...]).astype(BF16)
    h1 = (x_ref[...]
          + jnp.dot(n1, wout_ref[:SSM_WIDTH, :], preferred_element_type=F32)
          + jnp.dot(n2, wout_ref[SSM_WIDTH:, :], preferred_element_type=F32))
    h1_ref[...] = h1
    hn_ref[...] = _rms(h1, gf_ref[...]).astype(BF16)


def _mix_out(ys, ya, x, wglu, bglu, gs, ga, wout, gf):
    m, d = x.shape
    tm = min(256, m)
    row = lambda w: pl.BlockSpec((tm, w), lambda i: (i, 0))
    full = lambda a: pl.BlockSpec(a.shape, lambda i: (0, 0))
    return pl.pallas_call(
        _mix_out_kernel,
        grid=(m // tm,),
        in_specs=[row(SSM_WIDTH), row(ATTN_WIDTH), row(d), full(wglu), full(bglu), full(gs), full(ga),
                  full(wout), full(gf)],
        out_specs=[row(d), row(d)],
        out_shape=[jax.ShapeDtypeStruct((m, d), F32), jax.ShapeDtypeStruct((m, d), BF16)],
        compiler_params=pltpu.CompilerParams(
            dimension_semantics=("parallel",), vmem_limit_bytes=VMEM_LIMIT),
        name="mix_out",
    )(ys, ya, x, wglu, bglu, gs, ga, wout, gf)


def _ffn_kernel(hn_ref, halo_ref, h1_ref, wg_ref, wv_ref, cwg_ref, cwv_ref, cbg_ref, cbv_ref, wd_ref,
                gfin_ref, o_ref, xh_ref, upg_ref, upv_ref, acc_ref, *, tm):
    j = pl.program_id(1)

    @pl.when(j == 0)
    def _():
        xh_ref[:HALO, :] = halo_ref[0]
        xh_ref[HALO:, :] = hn_ref[...]
        acc_ref[...] = jnp.zeros_like(acc_ref)

    xh = xh_ref[...]
    upg_ref[...] = jnp.dot(xh, wg_ref[...], preferred_element_type=F32)
    upv_ref[...] = jnp.dot(xh, wv_ref[...], preferred_element_type=F32)

    def conv(up_ref, cw_ref, cb_ref):
        return (cw_ref[0:1, :] * up_ref[pl.ds(HALO - 2, tm), :]
                + cw_ref[1:2, :] * up_ref[pl.ds(HALO - 1, tm), :]
                + cw_ref[2:3, :] * up_ref[pl.ds(HALO, tm), :]
                + cb_ref[...])

    gate = conv(upg_ref, cwg_ref, cbg_ref)
    val = conv(upv_ref, cwv_ref, cbv_ref)
    a = (gate * _sigmoid(gate) * val).astype(BF16)
    acc_ref[...] += jnp.dot(a, wd_ref[...], preferred_element_type=F32)

    @pl.when(j == pl.num_programs(1) - 1)
    def _():
        o_ref[...] = _rms(h1_ref[...] + acc_ref[...], gfin_ref[...])


def _ffn(hn, halo, h1, wup, cw, cb, wd, gfin):
    m, d = hn.shape
    dff = wd.shape[0]
    tm = 512
    tf = 512
    nf = dff // tf
    kern = functools.partial(_ffn_kernel, tm=tm)
    return pl.pallas_call(
        kern,
        grid=(m // tm, nf),
        in_specs=[
            pl.BlockSpec((tm, d), lambda i, j: (i, 0)),
            pl.BlockSpec((1, HALO, d), lambda i, j: (i, 0, 0)),
            pl.BlockSpec((tm, d), lambda i, j: (i, 0)),
            pl.BlockSpec((d, tf), lambda i, j: (0, j)),
            pl.BlockSpec((d, tf), lambda i, j: (0, nf + j)),
            pl.BlockSpec((CONV_W, tf), lambda i, j: (0, j)),
            pl.BlockSpec((CONV_W, tf), lambda i, j: (0, nf + j)),
            pl.BlockSpec((1, tf), lambda i, j: (0, j)),
            pl.BlockSpec((1, tf), lambda i, j: (0, nf + j)),
            pl.BlockSpec((tf, d), lambda i, j: (j, 0)),
            pl.BlockSpec((1, d), lambda i, j: (0, 0)),
        ],
        out_specs=pl.BlockSpec((tm, d), lambda i, j: (i, 0)),
        out_shape=jax.ShapeDtypeStruct((m, d), F32),
        scratch_shapes=[
            pltpu.VMEM((tm + HALO, d), BF16),
            pltpu.VMEM((tm + HALO, tf), F32),
            pltpu.VMEM((tm + HALO, tf), F32),
            pltpu.VMEM((tm, d), F32),
        ],
        compiler_params=pltpu.CompilerParams(
            dimension_semantics=("parallel", "arbitrary"), vmem_limit_bytes=VMEM_LIMIT),
        name="conv_ffn",
    )(hn, halo, h1, wup, wup, cw, cw, cb, cb, wd, gfin)


def kernel(x, meta_tokens, norm_mix_g, w_in, ssm_lambda_re, ssm_lambda_im, ssm_log_dt, ssm_b_re, ssm_b_im, ssm_c_re, ssm_c_im, ssm_d, w_glu, b_glu, g_ssm_out, g_attn_out, w_out, norm_ffn_g, w_up, conv_w, conv_b, w_down, norm_final_g):
    assert w_in.shape[0] == 1, "single-layer block"
    nb, seq, d = x.shape
    row = lambda a: a.reshape(1, -1).astype(F32)
    m = nb * seq
    ffn_tile = 512
    assert seq % ffn_tile == 0

    w_in_b = w_in[0].astype(BF16)
    w_glu_b = w_glu[0].astype(BF16)
    w_out_b = w_out[0].astype(BF16)
    w_up_b = w_up[0].astype(BF16)
    w_down_b = w_down[0].astype(BF16)

    ar, ai, bbr, bbi = _ssm_discretize(ssm_lambda_re[0], ssm_lambda_im[0], ssm_log_dt[0],
                                       ssm_b_re[0], ssm_b_im[0])
    gl = GROUPS_PER_BLOCK
    per_block = lambda a: a.reshape((N_SSM_BLOCKS, gl) + a.shape[1:])
    to_in = lambda bb: _block_diag(jnp.swapaxes(per_block(bb), 2, 3))
    bblk = jnp.concatenate([to_in(bbr), to_in(bbi)], axis=-1).astype(BF16)
    to_out = lambda cc: _block_diag(jnp.swapaxes(per_block(cc), 2, 3)).astype(BF16)
    crblk = to_out(ssm_c_re[0].astype(F32))
    ciblk = to_out(ssm_c_im[0].astype(F32))
    ar_b = ar.reshape(N_SSM_BLOCKS, 1, STATE_LANES)
    ai_b = ai.reshape(N_SSM_BLOCKS, 1, STATE_LANES)
    dsk = ssm_d[0].astype(F32).reshape(N_SSM_BLOCKS, 1, LANES)

    rr = lax.broadcasted_iota(jnp.int32, (LANES, 2 * LANES), 0)
    cc = lax.broadcasted_iota(jnp.int32, (LANES, 2 * LANES), 1)
    uu = ((cc >= LANES) | (rr > cc)).astype(BF16)

    g_mix = row(norm_mix_g[0])

    proj_m = _norm_inproj(meta_tokens.astype(F32), g_mix, w_in_b)
    h0 = jnp.zeros((N_SSM_BLOCKS, 1, 2 * STATE_LANES), F32)
    ys_m, hfin_m = _ssm_scan(proj_m[None], h0, bblk, crblk, ciblk, ar_b, ai_b, dsk)
    proj_m_pad = jnp.pad(proj_m, ((LANES - N_META, 0), (0, 0)))
    kmeta = proj_m_pad[:, SSM_WIDTH + ATTN_WIDTH:SSM_WIDTH + 2 * ATTN_WIDTH]
    vmeta = proj_m_pad[:, SSM_WIDTH + 2 * ATTN_WIDTH:]
    ya_m = _attention(proj_m_pad[None], kmeta, vmeta, uu, has_prefix=False,
                      kmin=LANES - N_META)[0, LANES - N_META:]
    mix_w = (w_glu_b, row(b_glu[0]), row(g_ssm_out[0]), row(g_attn_out[0]), w_out_b, row(norm_ffn_g[0]))
    _, hn_m = _mix_out(ys_m[0], ya_m, meta_tokens.astype(F32), *mix_w)

    xf = x.reshape(m, d)
    proj = _norm_inproj(xf, g_mix, w_in_b).reshape(nb, seq, -1)
    ys, _ = _ssm_scan(proj, hfin_m, bblk, crblk, ciblk, ar_b, ai_b, dsk)
    ya = _attention(proj, kmeta, vmeta, uu, has_prefix=True, kmin=0)
    h1, hn = _mix_out(ys.reshape(m, -1), ya.reshape(m, -1), xf, *mix_w)

    tiles = hn.reshape(nb, seq // ffn_tile, ffn_tile, d)
    prev_tail = tiles[:, :-1, ffn_tile - HALO:, :]
    first = jnp.broadcast_to(hn_m[None, None], (nb, 1, HALO, d))
    halo = jnp.concatenate([first, prev_tail], axis=1).reshape(m // ffn_tile, HALO, d)

    out = _ffn(hn, halo, h1, w_up_b, conv_w[0].astype(F32), row(conv_b[0]), w_down_b, row(norm_final_g))
    return out.reshape(nb, seq, d)
```

```python
import functools

import jax
import jax.numpy as jnp
from jax import lax
from jax.experimental import pallas as pl
from jax.experimental.pallas import tpu as pltpu

F32 = jnp.float32
BF16 = jnp.bfloat16

N_META = 16
SSM_WIDTH = 1024
ATTN_WIDTH = 1024
SSM_GROUP_CH = 16
SSM_GROUPS = 64
SSM_STATE = 64
HEAD_DIM = 64
CONV_W = 3
RMS_EPS = 1e-6

LANES = 128
GROUPS_PER_BLOCK = LANES // SSM_GROUP_CH
N_SSM_BLOCKS = SSM_GROUPS // GROUPS_PER_BLOCK
STATE_LANES = GROUPS_PER_BLOCK * SSM_STATE
HALO = 16
VMEM_LIMIT = 56 * 1024 * 1024


def _rms(x, g):
    ms = jnp.mean(x * x, axis=-1, keepdims=True)
    return x * lax.rsqrt(ms + RMS_EPS) * g


def _sigmoid(x):
    return 1.0 / (1.0 + jnp.exp(-x))


def _norm_inproj_kernel(x_ref, g_ref, w_ref, o_ref, xn_ref):
    @pl.when(pl.program_id(1) == 0)
    def _():
        xn_ref[...] = _rms(x_ref[...], g_ref[...]).astype(BF16)

    o_ref[...] = jnp.dot(xn_ref[...], w_ref[...], preferred_element_type=F32).astype(o_ref.dtype)


def _norm_inproj(x, g, w):
    m, d = x.shape
    n = w.shape[1]
    tm = min(512, m)
    tn = 1024
    return pl.pallas_call(
        _norm_inproj_kernel,
        grid=(m // tm, n // tn),
        in_specs=[
            pl.BlockSpec((tm, d), lambda i, j: (i, 0)),
            pl.BlockSpec((1, d), lambda i, j: (0, 0)),
            pl.BlockSpec((d, tn), lambda i, j: (0, j)),
        ],
        out_specs=pl.BlockSpec((tm, tn), lambda i, j: (i, j)),
        out_shape=jax.ShapeDtypeStruct((m, n), BF16),
        scratch_shapes=[pltpu.VMEM((tm, d), BF16)],
        compiler_params=pltpu.CompilerParams(
            dimension_semantics=("parallel", "arbitrary"), vmem_limit_bytes=VMEM_LIMIT),
        name="norm_inproj",
    )(x, g, w)


def _ssm_discretize_kernel(lr_ref, li_ref, ldt_ref, bre_ref, bim_ref,
                           ar_ref, ai_ref, bbr_ref, bbi_ref):
    lr = jnp.minimum(lr_ref[...], -1e-4)
    li = li_ref[...]
    delta = jnp.exp(ldt_ref[...])
    mag = jnp.exp(lr * delta)
    ar = mag * jnp.cos(li * delta)
    ai = mag * jnp.sin(li * delta)
    den = lr * lr + li * li
    nr = ar - 1.0
    ni = ai
    fr = (nr * lr + ni * li) / den
    fi = (ni * lr - nr * li) / den
    bre = bre_ref[...]
    bim = bim_ref[...]
    ar_ref[...] = ar
    ai_ref[...] = ai
    bbr_ref[...] = fr * bre - fi * bim
    bbi_ref[...] = fr * bim + fi * bre


def _ssm_discretize(lam_re, lam_im, log_dt, b_re, b_im):
    g, p, h = b_re.shape
    rep = lambda a: jnp.repeat(a, h, axis=1)
    shape = jax.ShapeDtypeStruct((g, p * h), F32)
    ar, ai, bbr, bbi = pl.pallas_call(
        _ssm_discretize_kernel,
        out_shape=(shape, shape, shape, shape),
        name="ssm_discretize",
    )(rep(lam_re), rep(lam_im), jnp.broadcast_to(log_dt[:, None], (g, p * h)),
      b_re.reshape(g, p * h), b_im.reshape(g, p * h))
    return ar[:, ::h], ai[:, ::h], bbr.reshape(g, p, h), bbi.reshape(g, p, h)


def _block_diag(blocks):
    nb, gl, r, c = blocks.shape
    eye = jnp.eye(gl, dtype=blocks.dtype)
    return (blocks[:, :, :, None, :] * eye[None, :, None, :, None]).reshape(nb, gl * r, gl * c)


def _ssm_scan_kernel(u_ref, bb_ref, cr_ref, ci_ref, ar_ref, ai_ref, d_ref, h0_ref,
                     y_ref, hf_ref, bu_ref, hh_ref, h_ref, *, nb, tk, pitch):
    c = pl.program_id(1)
    sl = STATE_LANES

    @pl.when(c == 0)
    def _():
        h_ref[...] = jnp.broadcast_to(h0_ref[0], (nb, 2 * sl))

    u = u_ref[...].reshape(nb * tk, LANES)
    bu = jnp.dot(u, bb_ref[0], preferred_element_type=F32)
    nlb = sl // LANES
    for b in range(nb):
        for j in range(2 * nlb):
            bu_ref[j, b * pitch:b * pitch + tk, :] = bu[b * tk:(b + 1) * tk, j * LANES:(j + 1) * LANES]

    blk = lambda ref, j: jnp.broadcast_to(ref[0, :, j * LANES:(j + 1) * LANES], (nb, LANES))
    ar = [blk(ar_ref, j) for j in range(nlb)]
    ai = [blk(ai_ref, j) for j in range(nlb)]
    hr = [h_ref[:, j * LANES:(j + 1) * LANES] for j in range(nlb)]
    hi = [h_ref[:, sl + j * LANES:sl + (j + 1) * LANES] for j in range(nlb)]
    for k in range(tk):
        rows = pl.ds(k, nb, stride=pitch) if nb > 1 else pl.ds(k, 1)
        for j in range(nlb):
            br = bu_ref[j, rows, :]
            bi = bu_ref[nlb + j, rows, :]
            hr[j], hi[j] = ar[j] * hr[j] - ai[j] * hi[j] + br, ar[j] * hi[j] + ai[j] * hr[j] + bi
            hh_ref[j, rows, :] = hr[j]
            hh_ref[nlb + j, rows, :] = hi[j]
    for j in range(nlb):
        h_ref[:, j * LANES:(j + 1) * LANES] = hr[j]
        h_ref[:, sl + j * LANES:sl + (j + 1) * LANES] = hi[j]

    def states(j):
        return jnp.concatenate([hh_ref[j, b * pitch:b * pitch + tk, :] for b in range(nb)], axis=0).astype(BF16)

    y = d_ref[0] * u.astype(F32)
    for j in range(nlb):
        y = y + jnp.dot(states(j), cr_ref[0, j * LANES:(j + 1) * LANES, :], preferred_element_type=F32)
        y = y - jnp.dot(states(nlb + j), ci_ref[0, j * LANES:(j + 1) * LANES, :], preferred_element_type=F32)
    y_ref[...] = y.reshape(nb, tk, LANES).astype(y_ref.dtype)

    @pl.when(c == pl.num_programs(1) - 1)
    def _():
        hf_ref[0] = h_ref[...]


def _ssm_scan(proj3, h0, bblk, crblk, ciblk, ar, ai, dsk):
    nb, l, _ = proj3.shape
    tk = min(128, l)
    sl = STATE_LANES
    pitch = tk + 4
    assert tk % 8 == 0
    kern = functools.partial(_ssm_scan_kernel, nb=nb, tk=tk, pitch=pitch)
    return pl.pallas_call(
        kern,
        grid=(N_SSM_BLOCKS, l // tk),
        in_specs=[
            pl.BlockSpec((nb, tk, LANES), lambda s, c: (0, c, s)),
            pl.BlockSpec((1, LANES, 2 * sl), lambda s, c: (s, 0, 0)),
            pl.BlockSpec((1, sl, LANES), lambda s, c: (s, 0, 0)),
            pl.BlockSpec((1, sl, LANES), lambda s, c: (s, 0, 0)),
            pl.BlockSpec((1, 1, sl), lambda s, c: (s, 0, 0)),
            pl.BlockSpec((1, 1, sl), lambda s, c: (s, 0, 0)),
            pl.BlockSpec((1, 1, LANES), lambda s, c: (s, 0, 0)),
            pl.BlockSpec((1, 1, 2 * sl), lambda s, c: (s, 0, 0)),
        ],
        out_specs=[
            pl.BlockSpec((nb, tk, LANES), lambda s, c: (0, c, s)),
            pl.BlockSpec((1, nb, 2 * sl), lambda s, c: (s, 0, 0)),
        ],
        out_shape=[
            jax.ShapeDtypeStruct((nb, l, SSM_WIDTH), BF16),
            jax.ShapeDtypeStruct((N_SSM_BLOCKS, nb, 2 * sl), F32),
        ],
        scratch_shapes=[
            pltpu.VMEM((2 * sl // LANES, nb * pitch, LANES), F32),
            pltpu.VMEM((2 * sl // LANES, nb * pitch, LANES), F32),
            pltpu.VMEM((nb, 2 * sl), F32),
        ],
        compiler_params=pltpu.CompilerParams(
            dimension_semantics=("parallel", "arbitrary"), vmem_limit_bytes=VMEM_LIMIT),
        name="ssm_scan",
    )(proj3, bblk, crblk, ciblk, ar, ai, dsk, h0)


KEY_TILE = LANES
ROW_CHUNK = 2 * KEY_TILE


def _sb_scores(qm, k, valid):
    z = lax.dot_general(qm, k, (((1,), (1,)), ((), ())), preferred_element_type=F32)
    l1p = jnp.log(1.0 + jnp.exp(-jnp.abs(z)))
    ls = jnp.minimum(z, 0.0) - l1p
    lk = ls - z
    if valid is not None:
        lk = jnp.where(valid, lk, 0.0)
    hi = lk.astype(BF16)
    lo = (lk - hi.astype(F32)).astype(BF16)
    return ls, jnp.concatenate([hi, lo], axis=1)


def _sb_weights(ls, hl, uu, c, valid):
    tk = ls.shape[1]
    r = jnp.dot(hl, uu, preferred_element_type=F32)
    w = jnp.exp(ls + r[:, :tk] + c)
    if valid is not None:
        w = jnp.where(valid, w, 0.0)
    return c + r[:, tk:], w.astype(BF16)


def _attn_kernel(q_ref, k_ref, v_ref, km_ref, vm_ref, uu_ref, o_ref, qm_ref, c_ref, acc_ref,
                 lsd_ref, hld_ref, lsm_ref, hlm_ref, lsi_ref, hli_ref, *, nsub, has_prefix, kmin):
    qi = pl.program_id(2)
    tk = KEY_TILE
    tq = nsub * tk
    lane = lax.broadcasted_iota(jnp.int32, (tk, LANES), 1)
    for s in range(nsub):
        qs = q_ref[0, s * tk:(s + 1) * tk, :] * 0.125
        qm_ref[s * ROW_CHUNK:s * ROW_CHUNK + tk, :] = jnp.where(lane < HEAD_DIM, qs, jnp.zeros_like(qs))
        qm_ref[s * ROW_CHUNK + tk:(s + 1) * ROW_CHUNK, :] = jnp.where(lane >= HEAD_DIM, qs, jnp.zeros_like(qs))
    c_ref[...] = jnp.zeros_like(c_ref)
    acc_ref[...] = jnp.zeros_like(acc_ref)
    uu = uu_ref[...]

    col = lax.broadcasted_iota(jnp.int32, (ROW_CHUNK, tk), 1)
    row = lax.broadcasted_iota(jnp.int32, (ROW_CHUNK, tk), 0) & (tk - 1)
    key_ok = (col >= kmin) if kmin > 0 else None
    causal = (col < row) if key_ok is None else (col < row) & key_ok

    rows = lambda s: slice(s * ROW_CHUNK, (s + 1) * ROW_CHUNK)
    key_tile = lambda t: pl.ds(pl.multiple_of(t * tk, tk), tk)

    def scores(t, ls_ref, hl_ref, first_sub, mask_of):
        k = km_ref[...] if t is None else k_ref[0, key_tile(t), :]
        for s in range(first_sub, nsub):
            ls, hl = _sb_scores(qm_ref[rows(s), :], k, mask_of(s))
            ls_ref[rows(s), :] = ls
            hl_ref[rows(s), :] = hl

    def apply(t, ls_ref, hl_ref, first_sub, mask_of):
        v = vm_ref[...] if t is None else v_ref[0, key_tile(t), :]
        subs = range(first_sub, nsub)
        wts = {s: _sb_weights(ls_ref[rows(s), :], hl_ref[rows(s), :], uu, c_ref[rows(s), :], mask_of(s))
               for s in subs}
        for s in subs:
            c_ref[rows(s), :] = wts[s][0]
            acc_ref[rows(s), :] += jnp.dot(wts[s][1], v, preferred_element_type=F32)

    no_mask = lambda s: None
    meta_mask = lambda s: col >= (tk - N_META)
    diag_mask = lambda j: (lambda s: causal if s == j else key_ok)
    n_before = qi * nsub
    for j in range(nsub - 1, -1, -1):
        scores(n_before + j, lsd_ref.at[j], hld_ref.at[j], j, diag_mask(j))
    if has_prefix:
        scores(None, lsm_ref, hlm_ref, 0, meta_mask)
    scores(jnp.maximum(n_before - 1, 0), lsi_ref.at[0], hli_ref.at[0], 0, no_mask)
    for j in range(nsub - 1, -1, -1):
        apply(n_before + j, lsd_ref.at[j], hld_ref.at[j], j, diag_mask(j))

    def body(i, carry):
        t = n_before - 1 - 2 * i
        scores(t - 1, lsi_ref.at[1], hli_ref.at[1], 0, no_mask)
        apply(t, lsi_ref.at[0], hli_ref.at[0], 0, no_mask)
        scores(jnp.maximum(t - 2, 0), lsi_ref.at[0], hli_ref.at[0], 0, no_mask)
        apply(t - 1, lsi_ref.at[1], hli_ref.at[1], 0, no_mask)
        return carry

    lax.fori_loop(0, n_before // 2, body, 0)
    if has_prefix:
        apply(None, lsm_ref, hlm_ref, 0, meta_mask)

    for s in range(nsub):
        a = acc_ref[s * ROW_CHUNK:s * ROW_CHUNK + tk, :]
        b = acc_ref[s * ROW_CHUNK + tk:(s + 1) * ROW_CHUNK, :]
        o_ref[0, s * tk:(s + 1) * tk, :] = jnp.where(lane < HEAD_DIM, a, b).astype(o_ref.dtype)


def _attention(proj3, kmeta, vmeta, uu, *, has_prefix, kmin):
    nb, l, _ = proj3.shape
    nsub = min(4, l // KEY_TILE)
    tq = nsub * KEY_TILE
    m = nsub * ROW_CHUNK
    assert nsub % 2 == 0 or l == tq, "earlier key tiles are consumed in pairs"
    nhp = ATTN_WIDTH // LANES
    qoff, koff, voff = SSM_WIDTH // LANES, (SSM_WIDTH + ATTN_WIDTH) // LANES, (SSM_WIDTH + 2 * ATTN_WIDTH) // LANES
    kern = functools.partial(_attn_kernel, nsub=nsub, has_prefix=has_prefix, kmin=kmin)
    return pl.pallas_call(
        kern,
        grid=(nb, nhp, l // tq),
        in_specs=[
            pl.BlockSpec((1, tq, LANES), lambda b, h, i: (b, i, qoff + h)),
            pl.BlockSpec((1, l, LANES), lambda b, h, i: (b, 0, koff + h)),
            pl.BlockSpec((1, l, LANES), lambda b, h, i: (b, 0, voff + h)),
            pl.BlockSpec((KEY_TILE, LANES), lambda b, h, i: (0, h)),
            pl.BlockSpec((KEY_TILE, LANES), lambda b, h, i: (0, h)),
            pl.BlockSpec((2 * KEY_TILE, 2 * KEY_TILE), lambda b, h, i: (0, 0)),
        ],
        out_specs=pl.BlockSpec((1, tq, LANES), lambda b, h, i: (b, i, h)),
        out_shape=jax.ShapeDtypeStruct((nb, l, ATTN_WIDTH), BF16),
        scratch_shapes=[
            pltpu.VMEM((m, LANES), BF16),
            pltpu.VMEM((m, LANES), F32),
            pltpu.VMEM((m, LANES), F32),
            pltpu.VMEM((nsub, m, LANES), F32),
            pltpu.VMEM((nsub, m, 2 * LANES), BF16),
            pltpu.VMEM((m, LANES), F32),
            pltpu.VMEM((m, 2 * LANES), BF16),
            pltpu.VMEM((2, m, LANES), F32),
            pltpu.VMEM((2, m, 2 * LANES), BF16),
        ],
        compiler_params=pltpu.CompilerParams(
            dimension_semantics=("parallel", "parallel", "arbitrary"), vmem_limit_bytes=VMEM_LIMIT),
        name="sb_attention",
    )(proj3, proj3, proj3, kmeta, vmeta, uu)


def _mix_out_kernel(ys_ref, ya_ref, x_ref, wglu_ref, bglu_ref, gs_ref, ga_ref, wout_ref, gf_ref,
                    h1_ref, hn_ref):
    y = jax.nn.gelu(ys_ref[...].astype(F32))
    gate = jnp.dot(y.astype(BF16), wglu_ref[...], preferred_element_type=F32) + bglu_ref[...]
    y = y * _sigmoid(gate)
    n1 = _rms(y, gs_ref[...]).astype(BF16)
    n2 = _rms(ya_ref[...].astype(F32), ga_ref[...]).astype(BF16)
    h1 = (x_ref[...]
          + jnp.dot(n1, wout_ref[:SSM_WIDTH, :], preferred_element_type=F32)
          + jnp.dot(n2, wout_ref[SSM_WIDTH:, :], preferred_element_type=F32))
    h1_ref[...] = h1
    hn_ref[...] = _rms(h1, gf_ref[...]).astype(BF16)


def _mix_out(ys, ya, x, wglu, bglu, gs, ga, wout, gf):
    m, d = x.shape
    tm = min(256, m)
    row = lambda w: pl.BlockSpec((tm, w), lambda i: (i, 0))
    full = lambda a: pl.BlockSpec(a.shape, lambda i: (0, 0))
    return pl.pallas_call(
        _mix_out_kernel,
        grid=(m // tm,),
        in_specs=[row(SSM_WIDTH), row(ATTN_WIDTH), row(d), full(wglu), full(bglu), full(gs), full(ga),
                  full(wout), full(gf)],
        out_specs=[row(d), row(d)],
        out_shape=[jax.ShapeDtypeStruct((m, d), F32), jax.ShapeDtypeStruct((m, d), BF16)],
        compiler_params=pltpu.CompilerParams(
            dimension_semantics=("parallel",), vmem_limit_bytes=VMEM_LIMIT),
        name="mix_out",
    )(ys, ya, x, wglu, bglu, gs, ga, wout, gf)


def _ffn_kernel(hn_ref, halo_ref, h1_ref, wg_ref, wv_ref, cwg_ref, cwv_ref, cbg_ref, cbv_ref, wd_ref,
                gfin_ref, o_ref, xh_ref, upg_ref, upv_ref, acc_ref, *, tm):
    j = pl.program_id(1)

    @pl.when(j == 0)
    def _():
        xh_ref[:HALO, :] = halo_ref[0]
        xh_ref[HALO:, :] = hn_ref[...]
        acc_ref[...] = jnp.zeros_like(acc_ref)

    xh = xh_ref[...]
    upg_ref[...] = jnp.dot(xh, wg_ref[...], preferred_element_type=F32)
    upv_ref[...] = jnp.dot(xh, wv_ref[...], preferred_element_type=F32)

    def conv(up_ref, cw_ref, cb_ref):
        return (cw_ref[0:1, :] * up_ref[pl.ds(HALO - 2, tm), :]
                + cw_ref[1:2, :] * up_ref[pl.ds(HALO - 1, tm), :]
                + cw_ref[2:3, :] * up_ref[pl.ds(HALO, tm), :]
                + cb_ref[...])

    gate = conv(upg_ref, cwg_ref, cbg_ref)
    val = conv(upv_ref, cwv_ref, cbv_ref)
    a = (gate * _sigmoid(gate) * val).astype(BF16)
    acc_ref[...] += jnp.dot(a, wd_ref[...], preferred_element_type=F32)

    @pl.when(j == pl.num_programs(1) - 1)
    def _():
        o_ref[...] = _rms(h1_ref[...] + acc_ref[...], gfin_ref[...])


def _ffn(hn, halo, h1, wup, cw, cb, wd, gfin):
    m, d = hn.shape
    dff = wd.shape[0]
    tm = 512
    tf = 512
    nf = dff // tf
    kern = functools.partial(_ffn_kernel, tm=tm)
    return pl.pallas_call(
        kern,
        grid=(m // tm, nf),
        in_specs=[
            pl.BlockSpec((tm, d), lambda i, j: (i, 0)),
            pl.BlockSpec((1, HALO, d), lambda i, j: (i, 0, 0)),
            pl.BlockSpec((tm, d), lambda i, j: (i, 0)),
            pl.BlockSpec((d, tf), lambda i, j: (0, j)),
            pl.BlockSpec((d, tf), lambda i, j: (0, nf + j)),
            pl.BlockSpec((CONV_W, tf), lambda i, j: (0, j)),
            pl.BlockSpec((CONV_W, tf), lambda i, j: (0, nf + j)),
            pl.BlockSpec((1, tf), lambda i, j: (0, j)),
            pl.BlockSpec((1, tf), lambda i, j: (0, nf + j)),
            pl.BlockSpec((tf, d), lambda i, j: (j, 0)),
            pl.BlockSpec((1, d), lambda i, j: (0, 0)),
        ],
        out_specs=pl.BlockSpec((tm, d), lambda i, j: (i, 0)),
        out_shape=jax.ShapeDtypeStruct((m, d), F32),
        scratch_shapes=[
            pltpu.VMEM((tm + HALO, d), BF16),
            pltpu.VMEM((tm + HALO, tf), F32),
            pltpu.VMEM((tm + HALO, tf), F32),
            pltpu.VMEM((tm, d), F32),
        ],
        compiler_params=pltpu.CompilerParams(
            dimension_semantics=("parallel", "arbitrary"), vmem_limit_bytes=VMEM_LIMIT),
        name="conv_ffn",
    )(hn, halo, h1, wup, wup, cw, cw, cb, cb, wd, gfin)


def kernel(x, meta_tokens, norm_mix_g, w_in, ssm_lambda_re, ssm_lambda_im, ssm_log_dt, ssm_b_re, ssm_b_im, ssm_c_re, ssm_c_im, ssm_d, w_glu, b_glu, g_ssm_out, g_attn_out, w_out, norm_ffn_g, w_up, conv_w, conv_b, w_down, norm_final_g):
    assert w_in.shape[0] == 1, "single-layer block"
    nb, seq, d = x.shape
    row = lambda a: a.reshape(1, -1).astype(F32)
    m = nb * seq
    ffn_tile = 512
    assert seq % ffn_tile == 0

    w_in_b = w_in[0].astype(BF16)
    w_glu_b = w_glu[0].astype(BF16)
    w_out_b = w_out[0].astype(BF16)
    w_up_b = w_up[0].astype(BF16)
    w_down_b = w_down[0].astype(BF16)

    ar, ai, bbr, bbi = _ssm_discretize(ssm_lambda_re[0], ssm_lambda_im[0], ssm_log_dt[0],
                                       ssm_b_re[0], ssm_b_im[0])
    gl = GROUPS_PER_BLOCK
    per_block = lambda a: a.reshape((N_SSM_BLOCKS, gl) + a.shape[1:])
    to_in = lambda bb: _block_diag(jnp.swapaxes(per_block(bb), 2, 3))
    bblk = jnp.concatenate([to_in(bbr), to_in(bbi)], axis=-1).astype(BF16)
    to_out = lambda cc: _block_diag(jnp.swapaxes(per_block(cc), 2, 3)).astype(BF16)
    crblk = to_out(ssm_c_re[0].astype(F32))
    ciblk = to_out(ssm_c_im[0].astype(F32))
    ar_b = ar.reshape(N_SSM_BLOCKS, 1, STATE_LANES)
    ai_b = ai.reshape(N_SSM_BLOCKS, 1, STATE_LANES)
    dsk = ssm_d[0].astype(F32).reshape(N_SSM_BLOCKS, 1, LANES)

    rr = lax.broadcasted_iota(jnp.int32, (2 * KEY_TILE, 2 * KEY_TILE), 0) & (KEY_TILE - 1)
    cc = lax.broadcasted_iota(jnp.int32, (2 * KEY_TILE, 2 * KEY_TILE), 1)
    uu = ((cc >= KEY_TILE) | (rr > cc)).astype(BF16)

    g_mix = row(norm_mix_g[0])

    proj_m = _norm_inproj(meta_tokens.astype(F32), g_mix, w_in_b)
    h0 = jnp.zeros((N_SSM_BLOCKS, 1, 2 * STATE_LANES), F32)
    ys_m, hfin_m = _ssm_scan(proj_m[None], h0, bblk, crblk, ciblk, ar_b, ai_b, dsk)
    proj_m_pad = jnp.pad(proj_m, ((LANES - N_META, 0), (0, 0)))
    kmeta = proj_m_pad[:, SSM_WIDTH + ATTN_WIDTH:SSM_WIDTH + 2 * ATTN_WIDTH]
    vmeta = proj_m_pad[:, SSM_WIDTH + 2 * ATTN_WIDTH:]
    ya_m = _attention(proj_m_pad[None], kmeta, vmeta, uu, has_prefix=False,
                      kmin=LANES - N_META)[0, LANES - N_META:]
    mix_w = (w_glu_b, row(b_glu[0]), row(g_ssm_out[0]), row(g_attn_out[0]), w_out_b, row(norm_ffn_g[0]))
    _, hn_m = _mix_out(ys_m[0], ya_m, meta_tokens.astype(F32), *mix_w)

    xf = x.reshape(m, d)
    proj = _norm_inproj(xf, g_mix, w_in_b).reshape(nb, seq, -1)
    ys, _ = _ssm_scan(proj, hfin_m, bblk, crblk, ciblk, ar_b, ai_b, dsk)
    ya = _attention(proj, kmeta, vmeta, uu, has_prefix=True, kmin=0)
    h1, hn = _mix_out(ys.reshape(m, -1), ya.reshape(m, -1), xf, *mix_w)

    tiles = hn.reshape(nb, seq // ffn_tile, ffn_tile, d)
    prev_tail = tiles[:, :-1, ffn_tile - HALO:, :]
    first = jnp.broadcast_to(hn_m[None, None], (nb, 1, HALO, d))
    halo = jnp.concatenate([first, prev_tail], axis=1).reshape(m // ffn_tile, HALO, d)

    out = _ffn(hn, halo, h1, w_up_b, conv_w[0].astype(F32), row(conv_b[0]), w_down_b, row(norm_final_g))
    return out.reshape(nb, seq, d)
```

```python
import functools

import jax
import jax.numpy as jnp
from jax import lax
from jax.experimental import pallas as pl
from jax.experimental.pallas import tpu as pltpu

F32 = jnp.float32
BF16 = jnp.bfloat16

N_META = 16
SSM_WIDTH = 1024
ATTN_WIDTH = 1024
SSM_GROUP_CH = 16
SSM_GROUPS = 64
SSM_STATE = 64
HEAD_DIM = 64
CONV_W = 3
RMS_EPS = 1e-6

LANES = 128
GROUPS_PER_BLOCK = LANES // SSM_GROUP_CH
N_SSM_BLOCKS = SSM_GROUPS // GROUPS_PER_BLOCK
STATE_LANES = GROUPS_PER_BLOCK * SSM_STATE
HALO = 16
VMEM_LIMIT = 56 * 1024 * 1024


def _rms(x, g):
    ms = jnp.mean(x * x, axis=-1, keepdims=True)
    return x * lax.rsqrt(ms + RMS_EPS) * g


def _sigmoid(x):
    return 1.0 / (1.0 + jnp.exp(-x))


def _norm_inproj_kernel(x_ref, g_ref, w_ref, o_ref, xn_ref):
    @pl.when(pl.program_id(1) == 0)
    def _():
        xn_ref[...] = _rms(x_ref[...], g_ref[...]).astype(BF16)

    o_ref[...] = jnp.dot(xn_ref[...], w_ref[...], preferred_element_type=F32).astype(o_ref.dtype)


def _norm_inproj(x, g, w):
    m, d = x.shape
    n = w.shape[1]
    tm = min(512, m)
    tn = 1024
    return pl.pallas_call(
        _norm_inproj_kernel,
        grid=(m // tm, n // tn),
        in_specs=[
            pl.BlockSpec((tm, d), lambda i, j: (i, 0)),
            pl.BlockSpec((1, d), lambda i, j: (0, 0)),
            pl.BlockSpec((d, tn), lambda i, j: (0, j)),
        ],
        out_specs=pl.BlockSpec((tm, tn), lambda i, j: (i, j)),
        out_shape=jax.ShapeDtypeStruct((m, n), BF16),
        scratch_shapes=[pltpu.VMEM((tm, d), BF16)],
        compiler_params=pltpu.CompilerParams(
            dimension_semantics=("parallel", "arbitrary"), vmem_limit_bytes=VMEM_LIMIT),
        name="norm_inproj",
    )(x, g, w)


def _ssm_discretize_kernel(lr_ref, li_ref, ldt_ref, bre_ref, bim_ref,
                           ar_ref, ai_ref, bbr_ref, bbi_ref):
    lr = jnp.minimum(lr_ref[...], -1e-4)
    li = li_ref[...]
    delta = jnp.exp(ldt_ref[...])
    mag = jnp.exp(lr * delta)
    ar = mag * jnp.cos(li * delta)
    ai = mag * jnp.sin(li * delta)
    den = lr * lr + li * li
    nr = ar - 1.0
    ni = ai
    fr = (nr * lr + ni * li) / den
    fi = (ni * lr - nr * li) / den
    bre = bre_ref[...]
    bim = bim_ref[...]
    ar_ref[...] = ar
    ai_ref[...] = ai
    bbr_ref[...] = fr * bre - fi * bim
    bbi_ref[...] = fr * bim + fi * bre


def _ssm_discretize(lam_re, lam_im, log_dt, b_re, b_im):
    g, p, h = b_re.shape
    rep = lambda a: jnp.repeat(a, h, axis=1)
    shape = jax.ShapeDtypeStruct((g, p * h), F32)
    ar, ai, bbr, bbi = pl.pallas_call(
        _ssm_discretize_kernel,
        out_shape=(shape, shape, shape, shape),
        name="ssm_discretize",
    )(rep(lam_re), rep(lam_im), jnp.broadcast_to(log_dt[:, None], (g, p * h)),
      b_re.reshape(g, p * h), b_im.reshape(g, p * h))
    return ar[:, ::h], ai[:, ::h], bbr.reshape(g, p, h), bbi.reshape(g, p, h)


def _block_diag(blocks):
    nb, gl, r, c = blocks.shape
    eye = jnp.eye(gl, dtype=blocks.dtype)
    return (blocks[:, :, :, None, :] * eye[None, :, None, :, None]).reshape(nb, gl * r, gl * c)


def _ssm_scan_kernel(u_ref, bb_ref, cr_ref, ci_ref, ar_ref, ai_ref, d_ref, h0_ref,
                     y_ref, hf_ref, bu_ref, hh_ref, h_ref, *, nb, tk, pitch):
    c = pl.program_id(1)
    nlb = 2 * STATE_LANES // LANES
    half = nlb // 4
    lanes = lambda lb: slice(lb * LANES, (lb + 1) * LANES)
    tile = lambda ref, p: jnp.concatenate(
        [jnp.broadcast_to(ref[0, :, lanes(2 * p + q)], (nb, LANES)) for q in range(2)], axis=0)

    @pl.when(c == 0)
    def _():
        for p in range(2 * half):
            h_ref[p] = tile(h0_ref, p)

    u = u_ref[...].reshape(nb * tk, LANES)
    bu = jnp.dot(u, bb_ref[0], preferred_element_type=F32)
    for lb in range(nlb):
        p, q = divmod(lb, 2)
        for b in range(nb):
            r0 = (q * nb + b) * pitch
            bu_ref[p, r0:r0 + tk, :] = bu[b * tk:(b + 1) * tk, lanes(lb)]

    ar = [tile(ar_ref, p) for p in range(half)]
    ai = [tile(ai_ref, p) for p in range(half)]
    hr = [h_ref[p] for p in range(half)]
    hi = [h_ref[half + p] for p in range(half)]
    for k in range(tk):
        rows = pl.ds(k, 2 * nb, stride=pitch)
        for p in range(half):
            br = bu_ref[p, rows, :]
            bi = bu_ref[half + p, rows, :]
            hr[p], hi[p] = ar[p] * hr[p] - ai[p] * hi[p] + br, ar[p] * hi[p] + ai[p] * hr[p] + bi
            hh_ref[p, rows, :] = hr[p]
            hh_ref[half + p, rows, :] = hi[p]
    for p in range(half):
        h_ref[p] = hr[p]
        h_ref[half + p] = hi[p]

    def states(lb):
        p, q = divmod(lb, 2)
        return jnp.concatenate([hh_ref[p, (q * nb + b) * pitch:(q * nb + b) * pitch + tk, :]
                                for b in range(nb)], axis=0).astype(BF16)

    y = d_ref[0] * u.astype(F32)
    for j in range(nlb // 2):
        y = y + jnp.dot(states(j), cr_ref[0, lanes(j), :], preferred_element_type=F32)
        y = y - jnp.dot(states(nlb // 2 + j), ci_ref[0, lanes(j), :], preferred_element_type=F32)
    y_ref[...] = y.reshape(nb, tk, LANES).astype(y_ref.dtype)

    @pl.when(c == pl.num_programs(1) - 1)
    def _():
        for lb in range(nlb):
            p, q = divmod(lb, 2)
            hf_ref[0, :, lanes(lb)] = h_ref[p, q * nb:(q + 1) * nb, :]


def _ssm_scan(proj3, h0, bblk, crblk, ciblk, ar, ai, dsk):
    nb, l, _ = proj3.shape
    tk = next(t for t in (256, 128, l) if l % t == 0)
    sl = STATE_LANES
    pitch = tk + 4
    assert tk % 8 == 0
    kern = functools.partial(_ssm_scan_kernel, nb=nb, tk=tk, pitch=pitch)
    return pl.pallas_call(
        kern,
        grid=(N_SSM_BLOCKS, l // tk),
        in_specs=[
            pl.BlockSpec((nb, tk, LANES), lambda s, c: (0, c, s)),
            pl.BlockSpec((1, LANES, 2 * sl), lambda s, c: (s, 0, 0)),
            pl.BlockSpec((1, sl, LANES), lambda s, c: (s, 0, 0)),
            pl.BlockSpec((1, sl, LANES), lambda s, c: (s, 0, 0)),
            pl.BlockSpec((1, 1, sl), lambda s, c: (s, 0, 0)),
            pl.BlockSpec((1, 1, sl), lambda s, c: (s, 0, 0)),
            pl.BlockSpec((1, 1, LANES), lambda s, c: (s, 0, 0)),
            pl.BlockSpec((1, 1, 2 * sl), lambda s, c: (s, 0, 0)),
        ],
        out_specs=[
            pl.BlockSpec((nb, tk, LANES), lambda s, c: (0, c, s)),
            pl.BlockSpec((1, nb, 2 * sl), lambda s, c: (s, 0, 0)),
        ],
        out_shape=[
            jax.ShapeDtypeStruct((nb, l, SSM_WIDTH), BF16),
            jax.ShapeDtypeStruct((N_SSM_BLOCKS, nb, 2 * sl), F32),
        ],
        scratch_shapes=[
            pltpu.VMEM((sl // LANES, 2 * nb * pitch, LANES), F32),
            pltpu.VMEM((sl // LANES, 2 * nb * pitch, LANES), F32),
            pltpu.VMEM((sl // LANES, 2 * nb, LANES), F32),
        ],
        compiler_params=pltpu.CompilerParams(
            dimension_semantics=("parallel", "arbitrary"), vmem_limit_bytes=VMEM_LIMIT),
        name="ssm_scan",
    )(proj3, bblk, crblk, ciblk, ar, ai, dsk, h0)


KEY_TILE = LANES
ROW_CHUNK = 2 * KEY_TILE


MASKED = -1e30
NEGLIGIBLE_LOG_WEIGHT = -88.0


def _sb_prepare(jobs, uu):
    zs = [lax.dot_general(qm, k, (((1,), (1,)), ((), ())), preferred_element_type=F32) for qm, k, _ in jobs]
    lss, hls = [], []
    for z, (_, _, valid) in zip(zs, jobs):
        l1p = jnp.log(1.0 + jnp.exp(-jnp.abs(z)))
        ls = jnp.minimum(z, 0.0) - l1p
        lk = ls - z
        if valid is not None:
            lk = jnp.where(valid, lk, 0.0)
        hi = lk.astype(BF16)
        lo = (lk - hi.astype(F32)).astype(BF16)
        lss.append(ls)
        hls.append(jnp.concatenate([hi, lo], axis=1))
    rs = [jnp.dot(hl, uu, preferred_element_type=F32) for hl in hls]
    outs = []
    for ls, r, (_, _, valid) in zip(lss, rs, jobs):
        tk = ls.shape[1]
        pre = ls + r[:, :tk]
        if valid is not None:
            pre = jnp.where(valid, pre, MASKED)
        outs.append((pre, r[:, tk:]))
    return outs


def _attn_kernel(q_ref, k_ref, v_ref, km_ref, vm_ref, uu_ref, o_ref, qm_ref, c_ref, acc_ref,
                 pred_ref, rsd_ref, prem_ref, rsm_ref, prei_ref, rsi_ref, *, nsub, has_prefix, kmin):
    qi = pl.program_id(2)
    tk = KEY_TILE
    lane = lax.broadcasted_iota(jnp.int32, (tk, LANES), 1)
    for s in range(nsub):
        qs = q_ref[0, s * tk:(s + 1) * tk, :] * 0.125
        qm_ref[s * ROW_CHUNK:s * ROW_CHUNK + tk, :] = jnp.where(lane < HEAD_DIM, qs, jnp.zeros_like(qs))
        qm_ref[s * ROW_CHUNK + tk:(s + 1) * ROW_CHUNK, :] = jnp.where(lane >= HEAD_DIM, qs, jnp.zeros_like(qs))
    c_ref[...] = jnp.zeros_like(c_ref)
    acc_ref[...] = jnp.zeros_like(acc_ref)
    uu = uu_ref[...]

    col = lax.broadcasted_iota(jnp.int32, (ROW_CHUNK, tk), 1)
    row = lax.broadcasted_iota(jnp.int32, (ROW_CHUNK, tk), 0) & (tk - 1)
    key_ok = (col >= kmin) if kmin > 0 else None
    causal = (col < row) if key_ok is None else (col < row) & key_ok

    rows = lambda s: slice(s * ROW_CHUNK, (s + 1) * ROW_CHUNK)
    key_tile = lambda t: pl.ds(pl.multiple_of(t * tk, tk), tk)

    def prepare(tiles):
        jobs, dests = [], []
        for t, pre_ref, rs_ref, first_sub, mask_of in tiles:
            k = km_ref[...] if t is None else k_ref[0, key_tile(t), :]
            for s in range(first_sub, nsub):
                jobs.append((qm_ref[rows(s), :], k, mask_of(s)))
                dests.append((pre_ref, rs_ref, s))
        for (pre, rs), (pre_ref, rs_ref, s) in zip(_sb_prepare(jobs, uu), dests):
            pre_ref[rows(s), :] = pre
            rs_ref[rows(s), :] = rs

    def apply(t, pre_ref, rs_ref, first_sub):
        v = vm_ref[...] if t is None else v_ref[0, key_tile(t), :]
        for s in range(first_sub, nsub):
            c = c_ref[rows(s), :]
            w = jnp.exp(pre_ref[rows(s), :] + c).astype(BF16)
            c_ref[rows(s), :] = c + rs_ref[rows(s), :]
            acc_ref[rows(s), :] += jnp.dot(w, v, preferred_element_type=F32)

    no_mask = lambda s: None
    meta_mask = lambda s: col >= (tk - N_META)
    diag_mask = lambda j: (lambda s: causal if s == j else key_ok)
    n_before = qi * nsub
    inner = lambda i: (n_before - 1 - 2 * i, n_before - 2 - 2 * i)
    slot = lambda x: (prei_ref.at[x], rsi_ref.at[x])

    first = [jnp.maximum(t, 0) for t in inner(0)]
    prepare([(n_before + j, pred_ref.at[j], rsd_ref.at[j], j, diag_mask(j)) for j in range(nsub - 1, -1, -1)]
            + [(first[x], *slot(x), 0, no_mask) for x in range(2)])
    for j in range(nsub - 1, -1, -1):
        apply(n_before + j, pred_ref.at[j], rsd_ref.at[j], j)

    def more(carry):
        i, cmax = carry
        return (i < n_before // 2) & (cmax > NEGLIGIBLE_LOG_WEIGHT)

    def pair(carry):
        i, _ = carry
        ts = inner(i)

        @pl.when(i > 0)
        def _():
            prepare([(ts[x], *slot(x), 0, no_mask) for x in range(2)])

        for x in range(2):
            apply(ts[x], *slot(x), 0)
        return i + 1, jnp.max(c_ref[...])

    _, cmax = lax.while_loop(more, pair, (jnp.int32(0), jnp.max(c_ref[...])))
    if has_prefix:
        @pl.when(cmax > NEGLIGIBLE_LOG_WEIGHT)
        def _():
            prepare([(None, prem_ref, rsm_ref, 0, meta_mask)])
            apply(None, prem_ref, rsm_ref, 0)

    for s in range(nsub):
        a = acc_ref[s * ROW_CHUNK:s * ROW_CHUNK + tk, :]
        b = acc_ref[s * ROW_CHUNK + tk:(s + 1) * ROW_CHUNK, :]
        o_ref[0, s * tk:(s + 1) * tk, :] = jnp.where(lane < HEAD_DIM, a, b).astype(o_ref.dtype)


def _attention(proj3, kmeta, vmeta, uu, *, has_prefix, kmin):
    nb, l, _ = proj3.shape
    nsub = min(4, l // KEY_TILE)
    tq = nsub * KEY_TILE
    m = nsub * ROW_CHUNK
    assert nsub % 2 == 0 or l == tq, "earlier key tiles are consumed in pairs"
    nhp = ATTN_WIDTH // LANES
    qoff, koff, voff = SSM_WIDTH // LANES, (SSM_WIDTH + ATTN_WIDTH) // LANES, (SSM_WIDTH + 2 * ATTN_WIDTH) // LANES
    kern = functools.partial(_attn_kernel, nsub=nsub, has_prefix=has_prefix, kmin=kmin)
    return pl.pallas_call(
        kern,
        grid=(nb, nhp, l // tq),
        in_specs=[
            pl.BlockSpec((1, tq, LANES), lambda b, h, i: (b, i, qoff + h)),
            pl.BlockSpec((1, l, LANES), lambda b, h, i: (b, 0, koff + h)),
            pl.BlockSpec((1, l, LANES), lambda b, h, i: (b, 0, voff + h)),
            pl.BlockSpec((KEY_TILE, LANES), lambda b, h, i: (0, h)),
            pl.BlockSpec((KEY_TILE, LANES), lambda b, h, i: (0, h)),
            pl.BlockSpec((2 * KEY_TILE, 2 * KEY_TILE), lambda b, h, i: (0, 0)),
        ],
        out_specs=pl.BlockSpec((1, tq, LANES), lambda b, h, i: (b, i, h)),
        out_shape=jax.ShapeDtypeStruct((nb, l, ATTN_WIDTH), BF16),
        scratch_shapes=[
            pltpu.VMEM((m, LANES), BF16),
            pltpu.VMEM((m, LANES), F32),
            pltpu.VMEM((m, LANES), F32),
            pltpu.VMEM((nsub, m, LANES), F32),
            pltpu.VMEM((nsub, m, LANES), F32),
            pltpu.VMEM((m, LANES), F32),
            pltpu.VMEM((m, LANES), F32),
            pltpu.VMEM((2, m, LANES), F32),
            pltpu.VMEM((2, m, LANES), F32),
        ],
        compiler_params=pltpu.CompilerParams(
            dimension_semantics=("parallel", "parallel", "arbitrary"), vmem_limit_bytes=VMEM_LIMIT),
        name="sb_attention",
    )(proj3, proj3, proj3, kmeta, vmeta, uu)


def _mix_out_kernel(ys_ref, ya_ref, x_ref, wglu_ref, bglu_ref, gs_ref, ga_ref, wout_ref, gf_ref,
                    h1_ref, hn_ref):
    y = jax.nn.gelu(ys_ref[...].astype(F32))
    gate = jnp.dot(y.astype(BF16), wglu_ref[...], preferred_element_type=F32) + bglu_ref[...]
    y = y * _sigmoid(gate)
    n1 = _rms(y, gs_ref[...]).astype(BF16)
    n2 = _rms(ya_ref[...].astype(F32), ga_ref[...]).astype(BF16)
    h1 = (x_ref[...]
          + jnp.dot(n1, wout_ref[:SSM_WIDTH, :], preferred_element_type=F32)
          + jnp.dot(n2, wout_ref[SSM_WIDTH:, :], preferred_element_type=F32))
    h1_ref[...] = h1
    hn_ref[...] = _rms(h1, gf_ref[...]).astype(BF16)


def _mix_out(ys, ya, x, wglu, bglu, gs, ga, wout, gf):
    m, d = x.shape
    tm = min(256, m)
    row = lambda w: pl.BlockSpec((tm, w), lambda i: (i, 0))
    full = lambda a: pl.BlockSpec(a.shape, lambda i: (0, 0))
    return pl.pallas_call(
        _mix_out_kernel,
        grid=(m // tm,),
        in_specs=[row(SSM_WIDTH), row(ATTN_WIDTH), row(d), full(wglu), full(bglu), full(gs), full(ga),
                  full(wout), full(gf)],
        out_specs=[row(d), row(d)],
        out_shape=[jax.ShapeDtypeStruct((m, d), F32), jax.ShapeDtypeStruct((m, d), BF16)],
        compiler_params=pltpu.CompilerParams(
            dimension_semantics=("parallel",), vmem_limit_bytes=VMEM_LIMIT),
        name="mix_out",
    )(ys, ya, x, wglu, bglu, gs, ga, wout, gf)


def _ffn_kernel(hn_ref, halo_ref, h1_ref, wg_ref, wv_ref, cwg_ref, cwv_ref, cbg_ref, cbv_ref, wd_ref,
                gfin_ref, o_ref, xh_ref, upg_ref, upv_ref, acc_ref, *, tm):
    j = pl.program_id(1)

    @pl.when(j == 0)
    def _():
        xh_ref[:HALO, :] = halo_ref[0]
        xh_ref[HALO:, :] = hn_ref[...]
        acc_ref[...] = jnp.zeros_like(acc_ref)

    xh = xh_ref[...]
    upg_ref[...] = jnp.dot(xh, wg_ref[...], preferred_element_type=F32)
    upv_ref[...] = jnp.dot(xh, wv_ref[...], preferred_element_type=F32)

    def conv(up_ref, cw_ref, cb_ref):
        return (cw_ref[0:1, :] * up_ref[pl.ds(HALO - 2, tm), :]
                + cw_ref[1:2, :] * up_ref[pl.ds(HALO - 1, tm), :]
                + cw_ref[2:3, :] * up_ref[pl.ds(HALO, tm), :]
                + cb_ref[...])

    gate = conv(upg_ref, cwg_ref, cbg_ref)
    val = conv(upv_ref, cwv_ref, cbv_ref)
    a = (gate * _sigmoid(gate) * val).astype(BF16)
    acc_ref[...] += jnp.dot(a, wd_ref[...], preferred_element_type=F32)

    @pl.when(j == pl.num_programs(1) - 1)
    def _():
        o_ref[...] = _rms(h1_ref[...] + acc_ref[...], gfin_ref[...])


def _ffn(hn, halo, h1, wup, cw, cb, wd, gfin):
    m, d = hn.shape
    dff = wd.shape[0]
    tm = 512
    tf = 512
    nf = dff // tf
    kern = functools.partial(_ffn_kernel, tm=tm)
    return pl.pallas_call(
        kern,
        grid=(m // tm, nf),
        in_specs=[
            pl.BlockSpec((tm, d), lambda i, j: (i, 0)),
            pl.BlockSpec((1, HALO, d), lambda i, j: (i, 0, 0)),
            pl.BlockSpec((tm, d), lambda i, j: (i, 0)),
            pl.BlockSpec((d, tf), lambda i, j: (0, j)),
            pl.BlockSpec((d, tf), lambda i, j: (0, nf + j)),
            pl.BlockSpec((CONV_W, tf), lambda i, j: (0, j)),
            pl.BlockSpec((CONV_W, tf), lambda i, j: (0, nf + j)),
            pl.BlockSpec((1, tf), lambda i, j: (0, j)),
            pl.BlockSpec((1, tf), lambda i, j: (0, nf + j)),
            pl.BlockSpec((tf, d), lambda i, j: (j, 0)),
            pl.BlockSpec((1, d), lambda i, j: (0, 0)),
        ],
        out_specs=pl.BlockSpec((tm, d), lambda i, j: (i, 0)),
        out_shape=jax.ShapeDtypeStruct((m, d), F32),
        scratch_shapes=[
            pltpu.VMEM((tm + HALO, d), BF16),
            pltpu.VMEM((tm + HALO, tf), F32),
            pltpu.VMEM((tm + HALO, tf), F32),
            pltpu.VMEM((tm, d), F32),
        ],
        compiler_params=pltpu.CompilerParams(
            dimension_semantics=("parallel", "arbitrary"), vmem_limit_bytes=VMEM_LIMIT),
        name="conv_ffn",
    )(hn, halo, h1, wup, wup, cw, cw, cb, cb, wd, gfin)


def kernel(x, meta_tokens, norm_mix_g, w_in, ssm_lambda_re, ssm_lambda_im, ssm_log_dt, ssm_b_re, ssm_b_im, ssm_c_re, ssm_c_im, ssm_d, w_glu, b_glu, g_ssm_out, g_attn_out, w_out, norm_ffn_g, w_up, conv_w, conv_b, w_down, norm_final_g):
    assert w_in.shape[0] == 1, "single-layer block"
    nb, seq, d = x.shape
    row = lambda a: a.reshape(1, -1).astype(F32)
    m = nb * seq
    ffn_tile = 512
    assert seq % ffn_tile == 0

    w_in_b = w_in[0].astype(BF16)
    w_glu_b = w_glu[0].astype(BF16)
    w_out_b = w_out[0].astype(BF16)
    w_up_b = w_up[0].astype(BF16)
    w_down_b = w_down[0].astype(BF16)

    ar, ai, bbr, bbi = _ssm_discretize(ssm_lambda_re[0], ssm_lambda_im[0], ssm_log_dt[0],
                                       ssm_b_re[0], ssm_b_im[0])
    gl = GROUPS_PER_BLOCK
    per_block = lambda a: a.reshape((N_SSM_BLOCKS, gl) + a.shape[1:])
    to_in = lambda bb: _block_diag(jnp.swapaxes(per_block(bb), 2, 3))
    bblk = jnp.concatenate([to_in(bbr), to_in(bbi)], axis=-1).astype(BF16)
    to_out = lambda cc: _block_diag(jnp.swapaxes(per_block(cc), 2, 3)).astype(BF16)
    crblk = to_out(ssm_c_re[0].astype(F32))
    ciblk = to_out(ssm_c_im[0].astype(F32))
    ar_b = ar.reshape(N_SSM_BLOCKS, 1, STATE_LANES)
    ai_b = ai.reshape(N_SSM_BLOCKS, 1, STATE_LANES)
    dsk = ssm_d[0].astype(F32).reshape(N_SSM_BLOCKS, 1, LANES)

    rr = lax.broadcasted_iota(jnp.int32, (2 * KEY_TILE, 2 * KEY_TILE), 0) & (KEY_TILE - 1)
    cc = lax.broadcasted_iota(jnp.int32, (2 * KEY_TILE, 2 * KEY_TILE), 1)
    uu = ((cc >= KEY_TILE) | (rr > cc)).astype(BF16)

    g_mix = row(norm_mix_g[0])

    proj_m = _norm_inproj(meta_tokens.astype(F32), g_mix, w_in_b)
    h0 = jnp.zeros((N_SSM_BLOCKS, 1, 2 * STATE_LANES), F32)
    ys_m, hfin_m = _ssm_scan(proj_m[None], h0, bblk, crblk, ciblk, ar_b, ai_b, dsk)
    proj_m_pad = jnp.pad(proj_m, ((LANES - N_META, 0), (0, 0)))
    kmeta = proj_m_pad[:, SSM_WIDTH + ATTN_WIDTH:SSM_WIDTH + 2 * ATTN_WIDTH]
    vmeta = proj_m_pad[:, SSM_WIDTH + 2 * ATTN_WIDTH:]
    ya_m = _attention(proj_m_pad[None], kmeta, vmeta, uu, has_prefix=False,
                      kmin=LANES - N_META)[0, LANES - N_META:]
    mix_w = (w_glu_b, row(b_glu[0]), row(g_ssm_out[0]), row(g_attn_out[0]), w_out_b, row(norm_ffn_g[0]))
    _, hn_m = _mix_out(ys_m[0], ya_m, meta_tokens.astype(F32), *mix_w)

    xf = x.reshape(m, d)
    proj = _norm_inproj(xf, g_mix, w_in_b).reshape(nb, seq, -1)
    ys, _ = _ssm_scan(proj, hfin_m, bblk, crblk, ciblk, ar_b, ai_b, dsk)
    ya = _attention(proj, kmeta, vmeta, uu, has_prefix=True, kmin=0)
    h1, hn = _mix_out(ys.reshape(m, -1), ya.reshape(m, -1), xf, *mix_w)

    tiles = hn.reshape(nb, seq // ffn_tile, ffn_tile, d)
    prev_tail = tiles[:, :-1, ffn_tile - HALO:, :]
    first = jnp.broadcast_to(hn_m[None, None], (nb, 1, HALO, d))
    halo = jnp.concatenate([first, prev_tail], axis=1).reshape(m // ffn_tile, HALO, d)

    out = _ffn(hn, halo, h1, w_up_b, conv_w[0].astype(F32), row(conv_b[0]), w_down_b, row(norm_final_g))
    return out.reshape(nb, seq, d)
```

```python
import functools

import jax
import jax.numpy as jnp
from jax import lax
from jax.experimental import pallas as pl
from jax.experimental.pallas import tpu as pltpu

F32 = jnp.float32
BF16 = jnp.bfloat16

N_META = 16
SSM_WIDTH = 1024
ATTN_WIDTH = 1024
SSM_GROUP_CH = 16
SSM_GROUPS = 64
SSM_STATE = 64
HEAD_DIM = 64
CONV_W = 3
RMS_EPS = 1e-6

LANES = 128
GROUPS_PER_BLOCK = LANES // SSM_GROUP_CH
N_SSM_BLOCKS = SSM_GROUPS // GROUPS_PER_BLOCK
STATE_LANES = GROUPS_PER_BLOCK * SSM_STATE
HALO = 16
VMEM_LIMIT = 56 * 1024 * 1024


def _rms(x, g):
    ms = jnp.mean(x * x, axis=-1, keepdims=True)
    return x * lax.rsqrt(ms + RMS_EPS) * g


def _sigmoid(x):
    return 1.0 / (1.0 + jnp.exp(-x))


def _norm_inproj_kernel(x_ref, g_ref, w_ref, o_ref, xn_ref):
    @pl.when(pl.program_id(1) == 0)
    def _():
        xn_ref[...] = _rms(x_ref[...], g_ref[...]).astype(BF16)

    o_ref[...] = jnp.dot(xn_ref[...], w_ref[...], preferred_element_type=F32).astype(o_ref.dtype)


def _norm_inproj(x, g, w):
    m, d = x.shape
    n = w.shape[1]
    tm = min(512, m)
    tn = 2048
    return pl.pallas_call(
        _norm_inproj_kernel,
        grid=(m // tm, n // tn),
        in_specs=[
            pl.BlockSpec((tm, d), lambda i, j: (i, 0)),
            pl.BlockSpec((1, d), lambda i, j: (0, 0)),
            pl.BlockSpec((d, tn), lambda i, j: (0, j)),
        ],
        out_specs=pl.BlockSpec((tm, tn), lambda i, j: (i, j)),
        out_shape=jax.ShapeDtypeStruct((m, n), BF16),
        scratch_shapes=[pltpu.VMEM((tm, d), BF16)],
        compiler_params=pltpu.CompilerParams(
            dimension_semantics=("parallel", "arbitrary"), vmem_limit_bytes=VMEM_LIMIT),
        name="norm_inproj",
    )(x, g, w)


def _ssm_discretize_kernel(lr_ref, li_ref, ldt_ref, bre_ref, bim_ref,
                           ar_ref, ai_ref, bbr_ref, bbi_ref):
    lr = jnp.minimum(lr_ref[...], -1e-4)
    li = li_ref[...]
    delta = jnp.exp(ldt_ref[...])
    mag = jnp.exp(lr * delta)
    ar = mag * jnp.cos(li * delta)
    ai = mag * jnp.sin(li * delta)
    den = lr * lr + li * li
    nr = ar - 1.0
    ni = ai
    fr = (nr * lr + ni * li) / den
    fi = (ni * lr - nr * li) / den
    bre = bre_ref[...]
    bim = bim_ref[...]
    ar_ref[...] = ar
    ai_ref[...] = ai
    bbr_ref[...] = fr * bre - fi * bim
    bbi_ref[...] = fr * bim + fi * bre


def _ssm_discretize(lam_re, lam_im, log_dt, b_re, b_im):
    g, p, h = b_re.shape
    rep = lambda a: jnp.repeat(a, h, axis=1)
    shape = jax.ShapeDtypeStruct((g, p * h), F32)
    ar, ai, bbr, bbi = pl.pallas_call(
        _ssm_discretize_kernel,
        out_shape=(shape, shape, shape, shape),
        name="ssm_discretize",
    )(rep(lam_re), rep(lam_im), jnp.broadcast_to(log_dt[:, None], (g, p * h)),
      b_re.reshape(g, p * h), b_im.reshape(g, p * h))
    return ar[:, ::h], ai[:, ::h], bbr.reshape(g, p, h), bbi.reshape(g, p, h)


def _block_diag(blocks):
    nb, gl, r, c = blocks.shape
    eye = jnp.eye(gl, dtype=blocks.dtype)
    return (blocks[:, :, :, None, :] * eye[None, :, None, :, None]).reshape(nb, gl * r, gl * c)


def _ssm_scan_kernel(u_ref, bb_ref, cr_ref, ci_ref, ar_ref, ai_ref, d_ref, h0_ref,
                     y_ref, hf_ref, bu_ref, hh_ref, h_ref, *, nb, tk, pitch, npiece):
    c = pl.program_id(1)
    nlb = 2 * STATE_LANES // LANES
    half = nlb // 4
    lanes = lambda lb: slice(lb * LANES, (lb + 1) * LANES)
    tile = lambda ref, p: jnp.concatenate(
        [jnp.broadcast_to(ref[0, :, lanes(2 * p + q)], (nb, LANES)) for q in range(2)], axis=0)

    @pl.when(c == 0)
    def _():
        for p in range(2 * half):
            h_ref[p] = tile(h0_ref, p)

    ts = tk // npiece
    us = [u_ref[:, i * ts:(i + 1) * ts, :].reshape(nb * ts, LANES) for i in range(npiece)]
    for i in range(npiece):
        bu = jnp.dot(us[i], bb_ref[0], preferred_element_type=F32)
        for lb in range(nlb):
            p, q = divmod(lb, 2)
            for b in range(nb):
                r0 = (q * nb + b) * pitch + i * ts
                bu_ref[p, r0:r0 + ts, :] = bu[b * ts:(b + 1) * ts, lanes(lb)]

    ar = [tile(ar_ref, p) for p in range(half)]
    ai = [tile(ai_ref, p) for p in range(half)]
    hr = [h_ref[p] for p in range(half)]
    hi = [h_ref[half + p] for p in range(half)]

    def states(lb, i):
        p, q = divmod(lb, 2)
        return jnp.concatenate([hh_ref[p, (q * nb + b) * pitch + i * ts:(q * nb + b) * pitch + (i + 1) * ts, :]
                                for b in range(nb)], axis=0).astype(BF16)

    for i in range(npiece):
        for k in range(i * ts, (i + 1) * ts):
            rows = pl.ds(k, 2 * nb, stride=pitch)
            for p in range(half):
                br = bu_ref[p, rows, :]
                bi = bu_ref[half + p, rows, :]
                hr[p], hi[p] = ar[p] * hr[p] - ai[p] * hi[p] + br, ar[p] * hi[p] + ai[p] * hr[p] + bi
                hh_ref[p, rows, :] = hr[p]
                hh_ref[half + p, rows, :] = hi[p]
        y = d_ref[0] * us[i].astype(F32)
        for p in range(half):
            both = lambda lb0: jnp.concatenate([states(lb0 + 2 * p, i), states(lb0 + 2 * p + 1, i)], axis=1)
            krows = slice(2 * p * LANES, (2 * p + 2) * LANES)
            y = y + jnp.dot(both(0), cr_ref[0, krows, :], preferred_element_type=F32)
            y = y - jnp.dot(both(nlb // 2), ci_ref[0, krows, :], preferred_element_type=F32)
        y_ref[:, i * ts:(i + 1) * ts, :] = y.reshape(nb, ts, LANES).astype(y_ref.dtype)
    for p in range(half):
        h_ref[p] = hr[p]
        h_ref[half + p] = hi[p]

    @pl.when(c == pl.num_programs(1) - 1)
    def _():
        for lb in range(nlb):
            p, q = divmod(lb, 2)
            hf_ref[0, :, lanes(lb)] = h_ref[p, q * nb:(q + 1) * nb, :]


def _ssm_scan(proj3, h0, bblk, crblk, ciblk, ar, ai, dsk):
    nb, l, _ = proj3.shape
    tk = next(t for t in (256, 128, l) if l % t == 0)
    sl = STATE_LANES
    pitch = tk + 4
    assert tk % 8 == 0
    npiece = 4 if tk % 64 == 0 else 1
    kern = functools.partial(_ssm_scan_kernel, nb=nb, tk=tk, pitch=pitch, npiece=npiece)
    return pl.pallas_call(
        kern,
        grid=(N_SSM_BLOCKS, l // tk),
        in_specs=[
            pl.BlockSpec((nb, tk, LANES), lambda s, c: (0, c, s)),
            pl.BlockSpec((1, LANES, 2 * sl), lambda s, c: (s, 0, 0)),
            pl.BlockSpec((1, sl, LANES), lambda s, c: (s, 0, 0)),
            pl.BlockSpec((1, sl, LANES), lambda s, c: (s, 0, 0)),
            pl.BlockSpec((1, 1, sl), lambda s, c: (s, 0, 0)),
            pl.BlockSpec((1, 1, sl), lambda s, c: (s, 0, 0)),
            pl.BlockSpec((1, 1, LANES), lambda s, c: (s, 0, 0)),
            pl.BlockSpec((1, 1, 2 * sl), lambda s, c: (s, 0, 0)),
        ],
        out_specs=[
            pl.BlockSpec((nb, tk, LANES), lambda s, c: (0, c, s)),
            pl.BlockSpec((1, nb, 2 * sl), lambda s, c: (s, 0, 0)),
        ],
        out_shape=[
            jax.ShapeDtypeStruct((nb, l, SSM_WIDTH), BF16),
            jax.ShapeDtypeStruct((N_SSM_BLOCKS, nb, 2 * sl), F32),
        ],
        scratch_shapes=[
            pltpu.VMEM((sl // LANES, 2 * nb * pitch, LANES), F32),
            pltpu.VMEM((sl // LANES, 2 * nb * pitch, LANES), F32),
            pltpu.VMEM((sl // LANES, 2 * nb, LANES), F32),
        ],
        compiler_params=pltpu.CompilerParams(
            dimension_semantics=("parallel", "arbitrary"), vmem_limit_bytes=VMEM_LIMIT),
        name="ssm_scan",
    )(proj3, bblk, crblk, ciblk, ar, ai, dsk, h0)


KEY_TILE = LANES
ROW_CHUNK = 2 * KEY_TILE


MASKED = -1e30
NEGLIGIBLE_LOG_WEIGHT = -88.0


def _sb_prepare(jobs, uu):
    zs = [lax.dot_general(qm, k, (((1,), (1,)), ((), ())), preferred_element_type=F32) for qm, k, _ in jobs]
    lss, hls = [], []
    for z, (_, _, valid) in zip(zs, jobs):
        l1p = jnp.log(1.0 + jnp.exp(-jnp.abs(z)))
        ls = jnp.minimum(z, 0.0) - l1p
        lk = ls - z
        if valid is not None:
            lk = jnp.where(valid, lk, 0.0)
        lss.append(ls)
        hls.append(lk.astype(BF16))
    rs = [jnp.dot(hl, uu, preferred_element_type=F32) for hl in hls]
    outs = []
    for ls, r, (_, _, valid) in zip(lss, rs, jobs):
        tk = ls.shape[1]
        pre = ls + r[:, :tk]
        if valid is not None:
            pre = jnp.where(valid, pre, MASKED)
        outs.append((pre, r[:, tk:]))
    return outs


def _attn_kernel(q_ref, k_ref, v_ref, km_ref, vm_ref, uu_ref, o_ref, qm_ref, c_ref, acc_ref,
                 pred_ref, rsd_ref, prem_ref, rsm_ref, prei_ref, rsi_ref, *, nsub, has_prefix, kmin):
    qi = pl.program_id(2)
    tk = KEY_TILE
    lane = lax.broadcasted_iota(jnp.int32, (tk, LANES), 1)
    for s in range(nsub):
        qs = q_ref[0, s * tk:(s + 1) * tk, :] * 0.125
        qm_ref[s * ROW_CHUNK:s * ROW_CHUNK + tk, :] = jnp.where(lane < HEAD_DIM, qs, jnp.zeros_like(qs))
        qm_ref[s * ROW_CHUNK + tk:(s + 1) * ROW_CHUNK, :] = jnp.where(lane >= HEAD_DIM, qs, jnp.zeros_like(qs))
    c_ref[...] = jnp.zeros_like(c_ref)
    acc_ref[...] = jnp.zeros_like(acc_ref)
    uu = uu_ref[...]

    col = lax.broadcasted_iota(jnp.int32, (ROW_CHUNK, tk), 1)
    row = lax.broadcasted_iota(jnp.int32, (ROW_CHUNK, tk), 0) & (tk - 1)
    key_ok = (col >= kmin) if kmin > 0 else None
    causal = (col < row) if key_ok is None else (col < row) & key_ok

    rows = lambda s: slice(s * ROW_CHUNK, (s + 1) * ROW_CHUNK)
    key_tile = lambda t: pl.ds(pl.multiple_of(t * tk, tk), tk)

    def prepare(tiles):
        jobs, dests = [], []
        for t, pre_ref, rs_ref, first_sub, mask_of in tiles:
            k = km_ref[...] if t is None else k_ref[0, key_tile(t), :]
            for s in range(first_sub, nsub):
                jobs.append((qm_ref[rows(s), :], k, mask_of(s)))
                dests.append((pre_ref, rs_ref, s))
        for (pre, rs), (pre_ref, rs_ref, s) in zip(_sb_prepare(jobs, uu), dests):
            pre_ref[rows(s), :] = pre
            rs_ref[rows(s), :] = rs

    def apply(t, pre_ref, rs_ref, first_sub):
        v = vm_ref[...] if t is None else v_ref[0, key_tile(t), :]
        for s in range(first_sub, nsub):
            c = c_ref[rows(s), :]
            w = jnp.exp(pre_ref[rows(s), :] + c).astype(BF16)
            c_ref[rows(s), :] = c + rs_ref[rows(s), :]
            acc_ref[rows(s), :] += jnp.dot(w, v, preferred_element_type=F32)

    no_mask = lambda s: None
    meta_mask = lambda s: col >= (tk - N_META)
    diag_mask = lambda j: (lambda s: causal if s == j else key_ok)
    n_before = qi * nsub
    inner = lambda i: (n_before - 1 - 2 * i, n_before - 2 - 2 * i)
    slot = lambda x: (prei_ref.at[x], rsi_ref.at[x])

    first = [jnp.maximum(t, 0) for t in inner(0)]
    prepare([(n_before + j, pred_ref.at[j], rsd_ref.at[j], j, diag_mask(j)) for j in range(nsub - 1, -1, -1)]
            + [(first[x], *slot(x), 0, no_mask) for x in range(2)])
    for j in range(nsub - 1, -1, -1):
        apply(n_before + j, pred_ref.at[j], rsd_ref.at[j], j)

    def more(carry):
        i, cmax = carry
        return (i < n_before // 2) & (cmax > NEGLIGIBLE_LOG_WEIGHT)

    def pair(carry):
        i, _ = carry
        ts = inner(i)

        @pl.when(i > 0)
        def _():
            prepare([(ts[x], *slot(x), 0, no_mask) for x in range(2)])

        for x in range(2):
            apply(ts[x], *slot(x), 0)
        return i + 1, jnp.max(c_ref[...])

    _, cmax = lax.while_loop(more, pair, (jnp.int32(0), jnp.max(c_ref[...])))
    if has_prefix:
        @pl.when(cmax > NEGLIGIBLE_LOG_WEIGHT)
        def _():
            prepare([(None, prem_ref, rsm_ref, 0, meta_mask)])
            apply(None, prem_ref, rsm_ref, 0)

    for s in range(nsub):
        a = acc_ref[s * ROW_CHUNK:s * ROW_CHUNK + tk, :]
        b = acc_ref[s * ROW_CHUNK + tk:(s + 1) * ROW_CHUNK, :]
        o_ref[0, s * tk:(s + 1) * tk, :] = jnp.where(lane < HEAD_DIM, a, b).astype(o_ref.dtype)


def _attention(proj3, kmeta, vmeta, uu, *, has_prefix, kmin):
    nb, l, _ = proj3.shape
    nsub = min(4, l // KEY_TILE)
    tq = nsub * KEY_TILE
    m = nsub * ROW_CHUNK
    assert nsub % 2 == 0 or l == tq, "earlier key tiles are consumed in pairs"
    nhp = ATTN_WIDTH // LANES
    qoff, koff, voff = SSM_WIDTH // LANES, (SSM_WIDTH + ATTN_WIDTH) // LANES, (SSM_WIDTH + 2 * ATTN_WIDTH) // LANES
    kern = functools.partial(_attn_kernel, nsub=nsub, has_prefix=has_prefix, kmin=kmin)
    return pl.pallas_call(
        kern,
        grid=(nb, nhp, l // tq),
        in_specs=[
            pl.BlockSpec((1, tq, LANES), lambda b, h, i: (b, i, qoff + h)),
            pl.BlockSpec((1, l, LANES), lambda b, h, i: (b, 0, koff + h)),
            pl.BlockSpec((1, l, LANES), lambda b, h, i: (b, 0, voff + h)),
            pl.BlockSpec((KEY_TILE, LANES), lambda b, h, i: (0, h)),
            pl.BlockSpec((KEY_TILE, LANES), lambda b, h, i: (0, h)),
            pl.BlockSpec((KEY_TILE, 2 * KEY_TILE), lambda b, h, i: (0, 0)),
        ],
        out_specs=pl.BlockSpec((1, tq, LANES), lambda b, h, i: (b, i, h)),
        out_shape=jax.ShapeDtypeStruct((nb, l, ATTN_WIDTH), BF16),
        scratch_shapes=[
            pltpu.VMEM((m, LANES), BF16),
            pltpu.VMEM((m, LANES), F32),
            pltpu.VMEM((m, LANES), F32),
            pltpu.VMEM((nsub, m, LANES), F32),
            pltpu.VMEM((nsub, m, LANES), F32),
            pltpu.VMEM((m, LANES), F32),
            pltpu.VMEM((m, LANES), F32),
            pltpu.VMEM((2, m, LANES), F32),
            pltpu.VMEM((2, m, LANES), F32),
        ],
        compiler_params=pltpu.CompilerParams(
            dimension_semantics=("parallel", "parallel", "arbitrary"), vmem_limit_bytes=VMEM_LIMIT),
        name="sb_attention",
    )(proj3, proj3, proj3, kmeta, vmeta, uu)


def _mix_out_kernel(ys_ref, ya_ref, x_ref, wglu_ref, bglu_ref, gs_ref, ga_ref, wout_ref, gf_ref,
                    h1_ref, hn_ref):
    y = jax.nn.gelu(ys_ref[...].astype(F32))
    gate = jnp.dot(y.astype(BF16), wglu_ref[...], preferred_element_type=F32) + bglu_ref[...]
    y = y * _sigmoid(gate)
    n1 = _rms(y, gs_ref[...]).astype(BF16)
    n2 = _rms(ya_ref[...].astype(F32), ga_ref[...]).astype(BF16)
    h1 = (x_ref[...]
          + jnp.dot(n1, wout_ref[:SSM_WIDTH, :], preferred_element_type=F32)
          + jnp.dot(n2, wout_ref[SSM_WIDTH:, :], preferred_element_type=F32))
    h1_ref[...] = h1
    hn_ref[...] = _rms(h1, gf_ref[...]).astype(BF16)


def _mix_out(ys, ya, x, wglu, bglu, gs, ga, wout, gf):
    m, d = x.shape
    tm = min(512, m)
    row = lambda w: pl.BlockSpec((tm, w), lambda i: (i, 0))
    full = lambda a: pl.BlockSpec(a.shape, lambda i: (0, 0))
    return pl.pallas_call(
        _mix_out_kernel,
        grid=(m // tm,),
        in_specs=[row(SSM_WIDTH), row(ATTN_WIDTH), row(d), full(wglu), full(bglu), full(gs), full(ga),
                  full(wout), full(gf)],
        out_specs=[row(d), row(d)],
        out_shape=[jax.ShapeDtypeStruct((m, d), F32), jax.ShapeDtypeStruct((m, d), BF16)],
        compiler_params=pltpu.CompilerParams(
            dimension_semantics=("parallel",), vmem_limit_bytes=VMEM_LIMIT),
        name="mix_out",
    )(ys, ya, x, wglu, bglu, gs, ga, wout, gf)


def _ffn_kernel(hn_ref, halo_ref, h1_ref, wg_ref, wv_ref, cwg_ref, cwv_ref, cbg_ref, cbv_ref, wd_ref,
                gfin_ref, o_ref, xh_ref, upg_ref, upv_ref, *, tm):
    j = pl.program_id(1)

    @pl.when(j == 0)
    def _():
        xh_ref[:HALO, :] = halo_ref[0]
        xh_ref[HALO:, :] = hn_ref[...]
        o_ref[...] = jnp.zeros_like(o_ref)

    xh = xh_ref[...]
    upg_ref[...] = jnp.dot(xh, wg_ref[...], preferred_element_type=F32)
    upv_ref[...] = jnp.dot(xh, wv_ref[...], preferred_element_type=F32)

    def conv(up_ref, cw_ref, cb_ref):
        return (cw_ref[0:1, :] * up_ref[pl.ds(HALO - 2, tm), :]
                + cw_ref[1:2, :] * up_ref[pl.ds(HALO - 1, tm), :]
                + cw_ref[2:3, :] * up_ref[pl.ds(HALO, tm), :]
                + cb_ref[...])

    gate = conv(upg_ref, cwg_ref, cbg_ref)
    val = conv(upv_ref, cwv_ref, cbv_ref)
    a = (gate * _sigmoid(gate) * val).astype(BF16)
    o_ref[...] += jnp.dot(a, wd_ref[...], preferred_element_type=F32)

    @pl.when(j == pl.num_programs(1) - 1)
    def _():
        o_ref[...] = _rms(h1_ref[...] + o_ref[...], gfin_ref[...])


FFN_ROWS = 512


def _ffn(hn, halo, h1, wup, cw, cb, wd, gfin):
    m, d = hn.shape
    dff = wd.shape[0]
    tm = FFN_ROWS
    tf = 512
    nf = dff // tf
    kern = functools.partial(_ffn_kernel, tm=tm)
    return pl.pallas_call(
        kern,
        grid=(m // tm, nf),
        in_specs=[
            pl.BlockSpec((tm, d), lambda i, j: (i, 0)),
            pl.BlockSpec((1, HALO, d), lambda i, j: (i, 0, 0)),
            pl.BlockSpec((tm, d), lambda i, j: (i, 0)),
            pl.BlockSpec((d, tf), lambda i, j: (0, j)),
            pl.BlockSpec((d, tf), lambda i, j: (0, nf + j)),
            pl.BlockSpec((CONV_W, tf), lambda i, j: (0, j)),
            pl.BlockSpec((CONV_W, tf), lambda i, j: (0, nf + j)),
            pl.BlockSpec((1, tf), lambda i, j: (0, j)),
            pl.BlockSpec((1, tf), lambda i, j: (0, nf + j)),
            pl.BlockSpec((tf, d), lambda i, j: (j, 0)),
            pl.BlockSpec((1, d), lambda i, j: (0, 0)),
        ],
        out_specs=pl.BlockSpec((tm, d), lambda i, j: (i, 0)),
        out_shape=jax.ShapeDtypeStruct((m, d), F32),
        scratch_shapes=[
            pltpu.VMEM((tm + HALO, d), BF16),
            pltpu.VMEM((tm + HALO, tf), F32),
            pltpu.VMEM((tm + HALO, tf), F32),
        ],
        compiler_params=pltpu.CompilerParams(
            dimension_semantics=("parallel", "arbitrary"), vmem_limit_bytes=VMEM_LIMIT),
        name="conv_ffn",
    )(hn, halo, h1, wup, wup, cw, cw, cb, cb, wd, gfin)


def kernel(x, meta_tokens, norm_mix_g, w_in, ssm_lambda_re, ssm_lambda_im, ssm_log_dt, ssm_b_re, ssm_b_im, ssm_c_re, ssm_c_im, ssm_d, w_glu, b_glu, g_ssm_out, g_attn_out, w_out, norm_ffn_g, w_up, conv_w, conv_b, w_down, norm_final_g):
    assert w_in.shape[0] == 1, "single-layer block"
    nb, seq, d = x.shape
    row = lambda a: a.reshape(1, -1).astype(F32)
    m = nb * seq
    ffn_tile = FFN_ROWS
    assert seq % ffn_tile == 0

    w_in_b = w_in[0].astype(BF16)
    w_glu_b = w_glu[0].astype(BF16)
    w_out_b = w_out[0].astype(BF16)
    w_up_b = w_up[0].astype(BF16)
    w_down_b = w_down[0].astype(BF16)

    ar, ai, bbr, bbi = _ssm_discretize(ssm_lambda_re[0], ssm_lambda_im[0], ssm_log_dt[0],
                                       ssm_b_re[0], ssm_b_im[0])
    gl = GROUPS_PER_BLOCK
    per_block = lambda a: a.reshape((N_SSM_BLOCKS, gl) + a.shape[1:])
    to_in = lambda bb: _block_diag(jnp.swapaxes(per_block(bb), 2, 3))
    bblk = jnp.concatenate([to_in(bbr), to_in(bbi)], axis=-1).astype(BF16)
    to_out = lambda cc: _block_diag(jnp.swapaxes(per_block(cc), 2, 3)).astype(BF16)
    crblk = to_out(ssm_c_re[0].astype(F32))
    ciblk = to_out(ssm_c_im[0].astype(F32))
    ar_b = ar.reshape(N_SSM_BLOCKS, 1, STATE_LANES)
    ai_b = ai.reshape(N_SSM_BLOCKS, 1, STATE_LANES)
    dsk = ssm_d[0].astype(F32).reshape(N_SSM_BLOCKS, 1, LANES)

    rr = lax.broadcasted_iota(jnp.int32, (KEY_TILE, 2 * KEY_TILE), 0)
    cc = lax.broadcasted_iota(jnp.int32, (KEY_TILE, 2 * KEY_TILE), 1)
    uu = ((cc >= KEY_TILE) | (rr > cc)).astype(BF16)

    g_mix = row(norm_mix_g[0])

    proj_m = _norm_inproj(meta_tokens.astype(F32), g_mix, w_in_b)
    h0 = jnp.zeros((N_SSM_BLOCKS, 1, 2 * STATE_LANES), F32)
    ys_m, hfin_m = _ssm_scan(proj_m[None], h0, bblk, crblk, ciblk, ar_b, ai_b, dsk)
    proj_m_pad = jnp.pad(proj_m, ((LANES - N_META, 0), (0, 0)))
    kmeta = proj_m_pad[:, SSM_WIDTH + ATTN_WIDTH:SSM_WIDTH + 2 * ATTN_WIDTH]
    vmeta = proj_m_pad[:, SSM_WIDTH + 2 * ATTN_WIDTH:]
    ya_m = _attention(proj_m_pad[None], kmeta, vmeta, uu, has_prefix=False,
                      kmin=LANES - N_META)[0, LANES - N_META:]
    mix_w = (w_glu_b, row(b_glu[0]), row(g_ssm_out[0]), row(g_attn_out[0]), w_out_b, row(norm_ffn_g[0]))
    _, hn_m = _mix_out(ys_m[0], ya_m, meta_tokens.astype(F32), *mix_w)

    xf = x.reshape(m, d)
    proj = _norm_inproj(xf, g_mix, w_in_b).reshape(nb, seq, -1)
    ys, _ = _ssm_scan(proj, hfin_m, bblk, crblk, ciblk, ar_b, ai_b, dsk)
    ya = _attention(proj, kmeta, vmeta, uu, has_prefix=True, kmin=0)
    h1, hn = _mix_out(ys.reshape(m, -1), ya.reshape(m, -1), xf, *mix_w)

    tiles = hn.reshape(nb, seq // ffn_tile, ffn_tile, d)
    prev_tail = tiles[:, :-1, ffn_tile - HALO:, :]
    first = jnp.broadcast_to(hn_m[None, None], (nb, 1, HALO, d))
    halo = jnp.concatenate([first, prev_tail], axis=1).reshape(m // ffn_tile, HALO, d)

    out = _ffn(hn, halo, h1, w_up_b, conv_w[0].astype(F32), row(conv_b[0]), w_down_b, row(norm_final_g))
    return out.reshape(nb, seq, d)
```

```python
import functools

import jax
import jax.numpy as jnp
from jax import lax
from jax.experimental import pallas as pl
from jax.experimental.pallas import tpu as pltpu

F32 = jnp.float32
BF16 = jnp.bfloat16

N_META = 16
SSM_WIDTH = 1024
ATTN_WIDTH = 1024
SSM_GROUP_CH = 16
SSM_GROUPS = 64
SSM_STATE = 64
HEAD_DIM = 64
CONV_W = 3
RMS_EPS = 1e-6

LANES = 128
GROUPS_PER_BLOCK = LANES // SSM_GROUP_CH
N_SSM_BLOCKS = SSM_GROUPS // GROUPS_PER_BLOCK
STATE_LANES = GROUPS_PER_BLOCK * SSM_STATE
HALO = 16
VMEM_LIMIT = 56 * 1024 * 1024


def _rms(x, g):
    ms = jnp.mean(x * x, axis=-1, keepdims=True)
    return x * lax.rsqrt(ms + RMS_EPS) * g


def _sigmoid(x):
    return 1.0 / (1.0 + jnp.exp(-x))


def _norm_inproj_kernel(x_ref, g_ref, w_ref, o_ref, *, pieces):
    n = x_ref.shape[0] // pieces
    for p in range(pieces):
        rows = slice(p * n, (p + 1) * n)
        xn = _rms(x_ref[rows, :], g_ref[...]).astype(BF16)
        o_ref[rows, :] = jnp.dot(xn, w_ref[...], preferred_element_type=F32).astype(o_ref.dtype)


def _norm_inproj(x, g, w):
    m, d = x.shape
    n = w.shape[1]
    tm = min(512, m)
    kern = functools.partial(_norm_inproj_kernel, pieces=4 if tm % 64 == 0 else 1)
    return pl.pallas_call(
        kern,
        grid=(m // tm,),
        in_specs=[
            pl.BlockSpec((tm, d), lambda i: (i, 0)),
            pl.BlockSpec((1, d), lambda i: (0, 0)),
            pl.BlockSpec((d, n), lambda i: (0, 0), pipeline_mode=pl.Buffered(1)),
        ],
        out_specs=pl.BlockSpec((tm, n), lambda i: (i, 0)),
        out_shape=jax.ShapeDtypeStruct((m, n), BF16),
        compiler_params=pltpu.CompilerParams(
            dimension_semantics=("parallel",), vmem_limit_bytes=VMEM_LIMIT),
        name="norm_inproj",
    )(x, g, w)


def _ssm_discretize_kernel(lr_ref, li_ref, ldt_ref, bre_ref, bim_ref,
                           ar_ref, ai_ref, bbr_ref, bbi_ref):
    lr = jnp.minimum(lr_ref[...], -1e-4)
    li = li_ref[...]
    delta = jnp.exp(ldt_ref[...])
    mag = jnp.exp(lr * delta)
    ar = mag * jnp.cos(li * delta)
    ai = mag * jnp.sin(li * delta)
    den = lr * lr + li * li
    nr = ar - 1.0
    ni = ai
    fr = (nr * lr + ni * li) / den
    fi = (ni * lr - nr * li) / den
    bre = bre_ref[...]
    bim = bim_ref[...]
    ar_ref[...] = ar
    ai_ref[...] = ai
    bbr_ref[...] = fr * bre - fi * bim
    bbi_ref[...] = fr * bim + fi * bre


def _ssm_discretize(lam_re, lam_im, log_dt, b_re, b_im):
    g, p, h = b_re.shape
    rep = lambda a: jnp.repeat(a, h, axis=1)
    shape = jax.ShapeDtypeStruct((g, p * h), F32)
    ar, ai, bbr, bbi = pl.pallas_call(
        _ssm_discretize_kernel,
        out_shape=(shape, shape, shape, shape),
        name="ssm_discretize",
    )(rep(lam_re), rep(lam_im), jnp.broadcast_to(log_dt[:, None], (g, p * h)),
      b_re.reshape(g, p * h), b_im.reshape(g, p * h))
    return ar[:, ::h], ai[:, ::h], bbr.reshape(g, p, h), bbi.reshape(g, p, h)


def _block_diag(blocks):
    nb, gl, r, c = blocks.shape
    eye = jnp.eye(gl, dtype=blocks.dtype)
    return (blocks[:, :, :, None, :] * eye[None, :, None, :, None]).reshape(nb, gl * r, gl * c)


def _ssm_scan_kernel(u_ref, bb_ref, cr_ref, ci_ref, ar_ref, ai_ref, d_ref, h0_ref,
                     y_ref, hf_ref, bu_ref, hh_ref, h_ref, *, nb, tk, pitch, npiece):
    c = pl.program_id(1)
    nlb = 2 * STATE_LANES // LANES
    half = nlb // 4
    lanes = lambda lb: slice(lb * LANES, (lb + 1) * LANES)
    tile = lambda ref, p: jnp.concatenate(
        [jnp.broadcast_to(ref[0, :, lanes(2 * p + q)], (nb, LANES)) for q in range(2)], axis=0)

    @pl.when(c == 0)
    def _():
        for p in range(2 * half):
            h_ref[p] = tile(h0_ref, p)

    ts = tk // npiece
    us = [u_ref[:, i * ts:(i + 1) * ts, :].reshape(nb * ts, LANES) for i in range(npiece)]
    for i in range(npiece):
        bu = jnp.dot(us[i], bb_ref[0], preferred_element_type=F32)
        for lb in range(nlb):
            p, q = divmod(lb, 2)
            for b in range(nb):
                r0 = (q * nb + b) * pitch + i * ts
                bu_ref[p, r0:r0 + ts, :] = bu[b * ts:(b + 1) * ts, lanes(lb)]

    ar = [tile(ar_ref, p) for p in range(half)]
    ai = [tile(ai_ref, p) for p in range(half)]
    hr = [h_ref[p] for p in range(half)]
    hi = [h_ref[half + p] for p in range(half)]

    def states(lb, i):
        p, q = divmod(lb, 2)
        return jnp.concatenate([hh_ref[p, (q * nb + b) * pitch + i * ts:(q * nb + b) * pitch + (i + 1) * ts, :]
                                for b in range(nb)], axis=0).astype(BF16)

    for i in range(npiece):
        for k in range(i * ts, (i + 1) * ts):
            rows = pl.ds(k, 2 * nb, stride=pitch)
            for p in range(half):
                br = bu_ref[p, rows, :]
                bi = bu_ref[half + p, rows, :]
                hr[p], hi[p] = ar[p] * hr[p] - ai[p] * hi[p] + br, ar[p] * hi[p] + ai[p] * hr[p] + bi
                hh_ref[p, rows, :] = hr[p]
                hh_ref[half + p, rows, :] = hi[p]
        y = d_ref[0] * us[i].astype(F32)
        for p in range(half):
            both = lambda lb0: jnp.concatenate([states(lb0 + 2 * p, i), states(lb0 + 2 * p + 1, i)], axis=1)
            krows = slice(2 * p * LANES, (2 * p + 2) * LANES)
            y = y + jnp.dot(both(0), cr_ref[0, krows, :], preferred_element_type=F32)
            y = y - jnp.dot(both(nlb // 2), ci_ref[0, krows, :], preferred_element_type=F32)
        y_ref[:, i * ts:(i + 1) * ts, :] = y.reshape(nb, ts, LANES).astype(y_ref.dtype)
    for p in range(half):
        h_ref[p] = hr[p]
        h_ref[half + p] = hi[p]

    @pl.when(c == pl.num_programs(1) - 1)
    def _():
        for lb in range(nlb):
            p, q = divmod(lb, 2)
            hf_ref[0, :, lanes(lb)] = h_ref[p, q * nb:(q + 1) * nb, :]


def _ssm_scan(proj3, h0, bblk, crblk, ciblk, ar, ai, dsk):
    nb, l, _ = proj3.shape
    tk = next(t for t in (256, 128, l) if l % t == 0)
    sl = STATE_LANES
    pitch = tk + 4
    assert tk % 8 == 0
    npiece = 4 if tk % 64 == 0 else 1
    kern = functools.partial(_ssm_scan_kernel, nb=nb, tk=tk, pitch=pitch, npiece=npiece)
    return pl.pallas_call(
        kern,
        grid=(N_SSM_BLOCKS, l // tk),
        in_specs=[
            pl.BlockSpec((nb, tk, LANES), lambda s, c: (0, c, s)),
            pl.BlockSpec((1, LANES, 2 * sl), lambda s, c: (s, 0, 0)),
            pl.BlockSpec((1, sl, LANES), lambda s, c: (s, 0, 0)),
            pl.BlockSpec((1, sl, LANES), lambda s, c: (s, 0, 0)),
            pl.BlockSpec((1, 1, sl), lambda s, c: (s, 0, 0)),
            pl.BlockSpec((1, 1, sl), lambda s, c: (s, 0, 0)),
            pl.BlockSpec((1, 1, LANES), lambda s, c: (s, 0, 0)),
            pl.BlockSpec((1, 1, 2 * sl), lambda s, c: (s, 0, 0)),
        ],
        out_specs=[
            pl.BlockSpec((nb, tk, LANES), lambda s, c: (0, c, s)),
            pl.BlockSpec((1, nb, 2 * sl), lambda s, c: (s, 0, 0)),
        ],
        out_shape=[
            jax.ShapeDtypeStruct((nb, l, SSM_WIDTH), BF16),
            jax.ShapeDtypeStruct((N_SSM_BLOCKS, nb, 2 * sl), F32),
        ],
        scratch_shapes=[
            pltpu.VMEM((sl // LANES, 2 * nb * pitch, LANES), F32),
            pltpu.VMEM((sl // LANES, 2 * nb * pitch, LANES), F32),
            pltpu.VMEM((sl // LANES, 2 * nb, LANES), F32),
        ],
        compiler_params=pltpu.CompilerParams(
            dimension_semantics=("parallel", "arbitrary"), vmem_limit_bytes=VMEM_LIMIT),
        name="ssm_scan",
    )(proj3, bblk, crblk, ciblk, ar, ai, dsk, h0)


KEY_TILE = LANES
ROW_CHUNK = 2 * KEY_TILE


MASKED = -1e30
NEGLIGIBLE_LOG_WEIGHT = -88.0


def _sb_prepare(jobs, uu):
    zs = [lax.dot_general(qm, k, (((1,), (1,)), ((), ())), preferred_element_type=F32) for qm, k, _ in jobs]
    lss, hls = [], []
    for z, (_, _, valid) in zip(zs, jobs):
        l1p = jnp.log(1.0 + jnp.exp(-jnp.abs(z)))
        ls = jnp.minimum(z, 0.0) - l1p
        lk = ls - z
        if valid is not None:
            lk = jnp.where(valid, lk, 0.0)
        lss.append(ls)
        hls.append(lk.astype(BF16))
    rs = [jnp.dot(hl, uu, preferred_element_type=F32) for hl in hls]
    outs = []
    for ls, r, (_, _, valid) in zip(lss, rs, jobs):
        tk = ls.shape[1]
        pre = ls + r[:, :tk]
        if valid is not None:
            pre = jnp.where(valid, pre, MASKED)
        outs.append((pre, r[:, tk:]))
    return outs


def _attn_kernel(q_ref, k_ref, v_ref, km_ref, vm_ref, uu_ref, o_ref, qm_ref, c_ref, acc_ref,
                 pred_ref, rsd_ref, prem_ref, rsm_ref, prei_ref, rsi_ref, *, nsub, has_prefix, kmin):
    qi = pl.program_id(2)
    tk = KEY_TILE
    lane = lax.broadcasted_iota(jnp.int32, (tk, LANES), 1)
    for s in range(nsub):
        qs = q_ref[0, s * tk:(s + 1) * tk, :] * 0.125
        qm_ref[s * ROW_CHUNK:s * ROW_CHUNK + tk, :] = jnp.where(lane < HEAD_DIM, qs, jnp.zeros_like(qs))
        qm_ref[s * ROW_CHUNK + tk:(s + 1) * ROW_CHUNK, :] = jnp.where(lane >= HEAD_DIM, qs, jnp.zeros_like(qs))
    c_ref[...] = jnp.zeros_like(c_ref)
    acc_ref[...] = jnp.zeros_like(acc_ref)
    uu = uu_ref[...]

    col = lax.broadcasted_iota(jnp.int32, (ROW_CHUNK, tk), 1)
    row = lax.broadcasted_iota(jnp.int32, (ROW_CHUNK, tk), 0) & (tk - 1)
    key_ok = (col >= kmin) if kmin > 0 else None
    causal = (col < row) if key_ok is None else (col < row) & key_ok

    rows = lambda s: slice(s * ROW_CHUNK, (s + 1) * ROW_CHUNK)
    key_tile = lambda t: pl.ds(pl.multiple_of(t * tk, tk), tk)

    def prepare(tiles):
        jobs, dests = [], []
        for t, pre_ref, rs_ref, first_sub, mask_of in tiles:
            k = km_ref[...] if t is None else k_ref[0, key_tile(t), :]
            for s in range(first_sub, nsub):
                jobs.append((qm_ref[rows(s), :], k, mask_of(s)))
                dests.append((pre_ref, rs_ref, s))
        for (pre, rs), (pre_ref, rs_ref, s) in zip(_sb_prepare(jobs, uu), dests):
            pre_ref[rows(s), :] = pre
            rs_ref[rows(s), :] = rs

    def apply(t, pre_ref, rs_ref, first_sub):
        v = vm_ref[...] if t is None else v_ref[0, key_tile(t), :]
        for s in range(first_sub, nsub):
            c = c_ref[rows(s), :]
            w = jnp.exp(pre_ref[rows(s), :] + c).astype(BF16)
            c_ref[rows(s), :] = c + rs_ref[rows(s), :]
            acc_ref[rows(s), :] += jnp.dot(w, v, preferred_element_type=F32)

    no_mask = lambda s: None
    meta_mask = lambda s: col >= (tk - N_META)
    diag_mask = lambda j: (lambda s: causal if s == j else key_ok)
    n_before = qi * nsub
    inner = lambda i: (n_before - 1 - 2 * i, n_before - 2 - 2 * i)
    slot = lambda x: (prei_ref.at[x], rsi_ref.at[x])

    first = [jnp.maximum(t, 0) for t in inner(0)]
    prepare([(n_before + j, pred_ref.at[j], rsd_ref.at[j], j, diag_mask(j)) for j in range(nsub - 1, -1, -1)]
            + [(first[x], *slot(x), 0, no_mask) for x in range(2)])
    for j in range(nsub - 1, -1, -1):
        apply(n_before + j, pred_ref.at[j], rsd_ref.at[j], j)

    def more(carry):
        i, cmax = carry
        return (i < n_before // 2) & (cmax > NEGLIGIBLE_LOG_WEIGHT)

    def pair(carry):
        i, _ = carry
        ts = inner(i)

        @pl.when(i > 0)
        def _():
            prepare([(ts[x], *slot(x), 0, no_mask) for x in range(2)])

        for x in range(2):
            apply(ts[x], *slot(x), 0)
        return i + 1, jnp.max(c_ref[...])

    _, cmax = lax.while_loop(more, pair, (jnp.int32(0), jnp.max(c_ref[...])))
    if has_prefix:
        @pl.when(cmax > NEGLIGIBLE_LOG_WEIGHT)
        def _():
            prepare([(None, prem_ref, rsm_ref, 0, meta_mask)])
            apply(None, prem_ref, rsm_ref, 0)

    for s in range(nsub):
        a = acc_ref[s * ROW_CHUNK:s * ROW_CHUNK + tk, :]
        b = acc_ref[s * ROW_CHUNK + tk:(s + 1) * ROW_CHUNK, :]
        o_ref[0, s * tk:(s + 1) * tk, :] = jnp.where(lane < HEAD_DIM, a, b).astype(o_ref.dtype)


def _attention(proj3, kmeta, vmeta, uu, *, has_prefix, kmin):
    nb, l, _ = proj3.shape
    nsub = min(4, l // KEY_TILE)
    tq = nsub * KEY_TILE
    m = nsub * ROW_CHUNK
    assert nsub % 2 == 0 or l == tq, "earlier key tiles are consumed in pairs"
    nhp = ATTN_WIDTH // LANES
    qoff, koff, voff = SSM_WIDTH // LANES, (SSM_WIDTH + ATTN_WIDTH) // LANES, (SSM_WIDTH + 2 * ATTN_WIDTH) // LANES
    kern = functools.partial(_attn_kernel, nsub=nsub, has_prefix=has_prefix, kmin=kmin)
    return pl.pallas_call(
        kern,
        grid=(nb, nhp, l // tq),
        in_specs=[
            pl.BlockSpec((1, tq, LANES), lambda b, h, i: (b, i, qoff + h)),
            pl.BlockSpec((1, l, LANES), lambda b, h, i: (b, 0, koff + h)),
            pl.BlockSpec((1, l, LANES), lambda b, h, i: (b, 0, voff + h)),
            pl.BlockSpec((KEY_TILE, LANES), lambda b, h, i: (0, h)),
            pl.BlockSpec((KEY_TILE, LANES), lambda b, h, i: (0, h)),
            pl.BlockSpec((KEY_TILE, 2 * KEY_TILE), lambda b, h, i: (0, 0)),
        ],
        out_specs=pl.BlockSpec((1, tq, LANES), lambda b, h, i: (b, i, h)),
        out_shape=jax.ShapeDtypeStruct((nb, l, ATTN_WIDTH), BF16),
        scratch_shapes=[
            pltpu.VMEM((m, LANES), BF16),
            pltpu.VMEM((m, LANES), F32),
            pltpu.VMEM((m, LANES), F32),
            pltpu.VMEM((nsub, m, LANES), F32),
            pltpu.VMEM((nsub, m, LANES), F32),
            pltpu.VMEM((m, LANES), F32),
            pltpu.VMEM((m, LANES), F32),
            pltpu.VMEM((2, m, LANES), F32),
            pltpu.VMEM((2, m, LANES), F32),
        ],
        compiler_params=pltpu.CompilerParams(
            dimension_semantics=("parallel", "parallel", "arbitrary"), vmem_limit_bytes=VMEM_LIMIT),
        name="sb_attention",
    )(proj3, proj3, proj3, kmeta, vmeta, uu)


def _mix_out_kernel(ys_ref, ya_ref, x_ref, wglu_ref, bglu_ref, gs_ref, ga_ref, wout_ref, gf_ref,
                    h1_ref, hn_ref):
    y = jax.nn.gelu(ys_ref[...].astype(F32))
    gate = jnp.dot(y.astype(BF16), wglu_ref[...], preferred_element_type=F32) + bglu_ref[...]
    y = y * _sigmoid(gate)
    n1 = _rms(y, gs_ref[...]).astype(BF16)
    n2 = _rms(ya_ref[...].astype(F32), ga_ref[...]).astype(BF16)
    h1 = (x_ref[...]
          + jnp.dot(n1, wout_ref[:SSM_WIDTH, :], preferred_element_type=F32)
          + jnp.dot(n2, wout_ref[SSM_WIDTH:, :], preferred_element_type=F32))
    h1_ref[...] = h1
    hn_ref[...] = _rms(h1, gf_ref[...]).astype(BF16)


def _mix_out(ys, ya, x, wglu, bglu, gs, ga, wout, gf):
    m, d = x.shape
    tm = min(512, m)
    row = lambda w: pl.BlockSpec((tm, w), lambda i: (i, 0))
    full = lambda a: pl.BlockSpec(a.shape, lambda i: (0, 0))
    return pl.pallas_call(
        _mix_out_kernel,
        grid=(m // tm,),
        in_specs=[row(SSM_WIDTH), row(ATTN_WIDTH), row(d), full(wglu), full(bglu), full(gs), full(ga),
                  full(wout), full(gf)],
        out_specs=[row(d), row(d)],
        out_shape=[jax.ShapeDtypeStruct((m, d), F32), jax.ShapeDtypeStruct((m, d), BF16)],
        compiler_params=pltpu.CompilerParams(
            dimension_semantics=("parallel",), vmem_limit_bytes=VMEM_LIMIT),
        name="mix_out",
    )(ys, ya, x, wglu, bglu, gs, ga, wout, gf)


def _ffn_kernel(hn_ref, halo_ref, h1_ref, wg_ref, wv_ref, cwg_ref, cwv_ref, cbg_ref, cbv_ref, wd_ref,
                gfin_ref, o_ref, xh_ref, upg_ref, upv_ref, *, tm):
    j = pl.program_id(1)

    @pl.when(j == 0)
    def _():
        xh_ref[:HALO, :] = halo_ref[0]
        xh_ref[HALO:, :] = hn_ref[...]
        o_ref[...] = jnp.zeros_like(o_ref)

    n = tm // FFN_PIECES
    for h in range(FFN_PIECES):
        lo = 0 if h == 0 else h * n + HALO
        xs = xh_ref[lo:(h + 1) * n + HALO, :]
        upg_ref[lo:(h + 1) * n + HALO, :] = jnp.dot(xs, wg_ref[...], preferred_element_type=F32)
        upv_ref[lo:(h + 1) * n + HALO, :] = jnp.dot(xs, wv_ref[...], preferred_element_type=F32)

    def conv(up_ref, cw_ref, cb_ref, r0):
        return (cw_ref[0:1, :] * up_ref[pl.ds(r0 + HALO - 2, n), :]
                + cw_ref[1:2, :] * up_ref[pl.ds(r0 + HALO - 1, n), :]
                + cw_ref[2:3, :] * up_ref[pl.ds(r0 + HALO, n), :]
                + cb_ref[...])

    for h in range(FFN_PIECES):
        gate = conv(upg_ref, cwg_ref, cbg_ref, h * n)
        val = conv(upv_ref, cwv_ref, cbv_ref, h * n)
        a = (gate * _sigmoid(gate) * val).astype(BF16)
        o_ref[h * n:(h + 1) * n, :] += jnp.dot(a, wd_ref[...], preferred_element_type=F32)

    @pl.when(j == pl.num_programs(1) - 1)
    def _():
        o_ref[...] = _rms(h1_ref[...] + o_ref[...], gfin_ref[...])


FFN_ROWS = 512
FFN_PIECES = 2


def _ffn(hn, halo, h1, wup, cw, cb, wd, gfin):
    m, d = hn.shape
    dff = wd.shape[0]
    tm = FFN_ROWS
    tf = 512
    nf = dff // tf
    kern = functools.partial(_ffn_kernel, tm=tm)
    return pl.pallas_call(
        kern,
        grid=(m // tm, nf),
        in_specs=[
            pl.BlockSpec((tm, d), lambda i, j: (i, 0)),
            pl.BlockSpec((1, HALO, d), lambda i, j: (i, 0, 0)),
            pl.BlockSpec((tm, d), lambda i, j: (i, 0)),
            pl.BlockSpec((d, tf), lambda i, j: (0, j)),
            pl.BlockSpec((d, tf), lambda i, j: (0, nf + j)),
            pl.BlockSpec((CONV_W, tf), lambda i, j: (0, j)),
            pl.BlockSpec((CONV_W, tf), lambda i, j: (0, nf + j)),
            pl.BlockSpec((1, tf), lambda i, j: (0, j)),
            pl.BlockSpec((1, tf), lambda i, j: (0, nf + j)),
            pl.BlockSpec((tf, d), lambda i, j: (j, 0)),
            pl.BlockSpec((1, d), lambda i, j: (0, 0)),
        ],
        out_specs=pl.BlockSpec((tm, d), lambda i, j: (i, 0)),
        out_shape=jax.ShapeDtypeStruct((m, d), F32),
        scratch_shapes=[
            pltpu.VMEM((tm + HALO, d), BF16),
            pltpu.VMEM((tm + HALO, tf), F32),
            pltpu.VMEM((tm + HALO, tf), F32),
        ],
        compiler_params=pltpu.CompilerParams(
            dimension_semantics=("parallel", "arbitrary"), vmem_limit_bytes=VMEM_LIMIT),
        name="conv_ffn",
    )(hn, halo, h1, wup, wup, cw, cw, cb, cb, wd, gfin)


def kernel(x, meta_tokens, norm_mix_g, w_in, ssm_lambda_re, ssm_lambda_im, ssm_log_dt, ssm_b_re, ssm_b_im, ssm_c_re, ssm_c_im, ssm_d, w_glu, b_glu, g_ssm_out, g_attn_out, w_out, norm_ffn_g, w_up, conv_w, conv_b, w_down, norm_final_g):
    assert w_in.shape[0] == 1, "single-layer block"
    nb, seq, d = x.shape
    m = nb * seq
    ffn_tile = FFN_ROWS
    assert seq % ffn_tile == 0

    w_in_b = w_in[0].astype(BF16)
    w_glu_b = w_glu[0].astype(BF16)
    w_out_b = w_out[0].astype(BF16)
    w_up_b = w_up[0].astype(BF16)
    w_down_b = w_down[0].astype(BF16)

    ar, ai, bbr, bbi = _ssm_discretize(ssm_lambda_re[0], ssm_lambda_im[0], ssm_log_dt[0],
                                       ssm_b_re[0], ssm_b_im[0])
    gl = GROUPS_PER_BLOCK
    per_block = lambda a: a.reshape((N_SSM_BLOCKS, gl) + a.shape[1:])
    to_in = lambda bb: _block_diag(jnp.swapaxes(per_block(bb), 2, 3))
    bblk = jnp.concatenate([to_in(bbr), to_in(bbi)], axis=-1).astype(BF16)
    to_out = lambda cc: _block_diag(jnp.swapaxes(per_block(cc), 2, 3)).astype(BF16)
    crblk = to_out(ssm_c_re[0].astype(F32))
    ciblk = to_out(ssm_c_im[0].astype(F32))
    ar_b = ar.reshape(N_SSM_BLOCKS, 1, STATE_LANES)
    ai_b = ai.reshape(N_SSM_BLOCKS, 1, STATE_LANES)
    dsk = ssm_d[0].astype(F32).reshape(N_SSM_BLOCKS, 1, LANES)

    rr = lax.broadcasted_iota(jnp.int32, (KEY_TILE, 2 * KEY_TILE), 0)
    cc = lax.broadcasted_iota(jnp.int32, (KEY_TILE, 2 * KEY_TILE), 1)
    uu = ((cc >= KEY_TILE) | (rr > cc)).astype(BF16)

    g_mix = norm_mix_g

    proj_m = _norm_inproj(meta_tokens.astype(F32), g_mix, w_in_b)
    h0 = jnp.zeros((N_SSM_BLOCKS, 1, 2 * STATE_LANES), F32)
    ys_m, hfin_m = _ssm_scan(proj_m[None], h0, bblk, crblk, ciblk, ar_b, ai_b, dsk)
    proj_m_pad = jnp.pad(proj_m, ((LANES - N_META, 0), (0, 0)))
    kmeta = proj_m_pad[:, SSM_WIDTH + ATTN_WIDTH:SSM_WIDTH + 2 * ATTN_WIDTH]
    vmeta = proj_m_pad[:, SSM_WIDTH + 2 * ATTN_WIDTH:]
    ya_m = _attention(proj_m_pad[None], kmeta, vmeta, uu, has_prefix=False,
                      kmin=LANES - N_META)[0, LANES - N_META:]
    mix_w = (w_glu_b, b_glu, g_ssm_out, g_attn_out, w_out_b, norm_ffn_g)
    _, hn_m = _mix_out(ys_m[0], ya_m, meta_tokens.astype(F32), *mix_w)

    xf = x.reshape(m, d)
    proj = _norm_inproj(xf, g_mix, w_in_b).reshape(nb, seq, -1)
    ys, _ = _ssm_scan(proj, hfin_m, bblk, crblk, ciblk, ar_b, ai_b, dsk)
    ya = _attention(proj, kmeta, vmeta, uu, has_prefix=True, kmin=0)
    h1, hn = _mix_out(ys.reshape(m, -1), ya.reshape(m, -1), xf, *mix_w)

    tiles = hn.reshape(nb, seq // ffn_tile, ffn_tile, d)
    prev_tail = tiles[:, :-1, ffn_tile - HALO:, :]
    first = jnp.broadcast_to(hn_m[None, None], (nb, 1, HALO, d))
    halo = jnp.concatenate([first, prev_tail], axis=1).reshape(m // ffn_tile, HALO, d)

    out = _ffn(hn, halo, h1, w_up_b, conv_w[0], conv_b, w_down_b, norm_final_g.reshape(1, d))
    return out.reshape(nb, seq, d)
```

```python
import functools

import jax
import jax.numpy as jnp
from jax import lax
from jax.experimental import pallas as pl
from jax.experimental.pallas import tpu as pltpu

F32 = jnp.float32
BF16 = jnp.bfloat16

N_META = 16
SSM_WIDTH = 1024
ATTN_WIDTH = 1024
SSM_GROUP_CH = 16
SSM_GROUPS = 64
SSM_STATE = 64
HEAD_DIM = 64
CONV_W = 3
RMS_EPS = 1e-6

LANES = 128
GROUPS_PER_BLOCK = LANES // SSM_GROUP_CH
N_SSM_BLOCKS = SSM_GROUPS // GROUPS_PER_BLOCK
STATE_LANES = GROUPS_PER_BLOCK * SSM_STATE
HALO = 16
VMEM_LIMIT = 56 * 1024 * 1024


def _rms(x, g):
    ms = jnp.mean(x * x, axis=-1, keepdims=True)
    return x * lax.rsqrt(ms + RMS_EPS) * g


def _sigmoid(x):
    return 1.0 / (1.0 + jnp.exp(-x))


def _norm_inproj_kernel(x_ref, g_ref, w_ref, o_ref, *, pieces):
    n = x_ref.shape[0] // pieces
    for p in range(pieces):
        rows = slice(p * n, (p + 1) * n)
        xn = _rms(x_ref[rows, :], g_ref[...]).astype(BF16)
        o_ref[rows, :] = jnp.dot(xn, w_ref[...], preferred_element_type=F32).astype(o_ref.dtype)


def _norm_inproj(x, g, w):
    m, d = x.shape
    n = w.shape[1]
    tm = min(512, m)
    kern = functools.partial(_norm_inproj_kernel, pieces=4 if tm % 64 == 0 else 1)
    return pl.pallas_call(
        kern,
        grid=(m // tm,),
        in_specs=[
            pl.BlockSpec((tm, d), lambda i: (i, 0)),
            pl.BlockSpec((1, d), lambda i: (0, 0)),
            pl.BlockSpec((d, n), lambda i: (0, 0), pipeline_mode=pl.Buffered(1)),
        ],
        out_specs=pl.BlockSpec((tm, n), lambda i: (i, 0)),
        out_shape=jax.ShapeDtypeStruct((m, n), BF16),
        compiler_params=pltpu.CompilerParams(
            dimension_semantics=("parallel",), vmem_limit_bytes=VMEM_LIMIT),
        name="norm_inproj",
    )(x, g, w)


def _ssm_discretize_kernel(lr_ref, li_ref, ldt_ref, bre_ref, bim_ref,
                           ar_ref, ai_ref, bbr_ref, bbi_ref):
    lr = jnp.minimum(lr_ref[...], -1e-4)
    li = li_ref[...]
    delta = jnp.exp(ldt_ref[...])
    mag = jnp.exp(lr * delta)
    ar = mag * jnp.cos(li * delta)
    ai = mag * jnp.sin(li * delta)
    den = lr * lr + li * li
    nr = ar - 1.0
    ni = ai
    fr = (nr * lr + ni * li) / den
    fi = (ni * lr - nr * li) / den
    bre = bre_ref[...]
    bim = bim_ref[...]
    ar_ref[...] = ar
    ai_ref[...] = ai
    bbr_ref[...] = fr * bre - fi * bim
    bbi_ref[...] = fr * bim + fi * bre


def _ssm_discretize(lam_re, lam_im, log_dt, b_re, b_im):
    g, p, h = b_re.shape
    rep = lambda a: jnp.repeat(a, h, axis=1)
    shape = jax.ShapeDtypeStruct((g, p * h), F32)
    ar, ai, bbr, bbi = pl.pallas_call(
        _ssm_discretize_kernel,
        out_shape=(shape, shape, shape, shape),
        name="ssm_discretize",
    )(rep(lam_re), rep(lam_im), jnp.broadcast_to(log_dt[:, None], (g, p * h)),
      b_re.reshape(g, p * h), b_im.reshape(g, p * h))
    return ar[:, ::h], ai[:, ::h], bbr.reshape(g, p, h), bbi.reshape(g, p, h)


def _block_diag(blocks):
    nb, gl, r, c = blocks.shape
    eye = jnp.eye(gl, dtype=blocks.dtype)
    return (blocks[:, :, :, None, :] * eye[None, :, None, :, None]).reshape(nb, gl * r, gl * c)


def _ssm_scan_kernel(u_ref, bb_ref, cr_ref, ci_ref, ar_ref, ai_ref, d_ref, h0_ref,
                     y_ref, hf_ref, bu_ref, hh_ref, h_ref, *, nb, tk, pitch, npiece):
    c = pl.program_id(1)
    nlb = 2 * STATE_LANES // LANES
    half = nlb // 4
    lanes = lambda lb: slice(lb * LANES, (lb + 1) * LANES)
    tile = lambda ref, p: jnp.concatenate(
        [jnp.broadcast_to(ref[0, :, lanes(2 * p + q)], (nb, LANES)) for q in range(2)], axis=0)

    @pl.when(c == 0)
    def _():
        for p in range(2 * half):
            h_ref[p] = tile(h0_ref, p)

    ts = tk // npiece
    us = [u_ref[:, i * ts:(i + 1) * ts, :].reshape(nb * ts, LANES) for i in range(npiece)]
    for i in range(npiece):
        bu = jnp.dot(us[i], bb_ref[0], preferred_element_type=F32)
        for lb in range(nlb):
            p, q = divmod(lb, 2)
            for b in range(nb):
                r0 = (q * nb + b) * pitch + i * ts
                bu_ref[p, r0:r0 + ts, :] = bu[b * ts:(b + 1) * ts, lanes(lb)]

    ar = [tile(ar_ref, p) for p in range(half)]
    ai = [tile(ai_ref, p) for p in range(half)]
    hr = [h_ref[p] for p in range(half)]
    hi = [h_ref[half + p] for p in range(half)]

    def states(lb, i):
        p, q = divmod(lb, 2)
        return jnp.concatenate([hh_ref[p, (q * nb + b) * pitch + i * ts:(q * nb + b) * pitch + (i + 1) * ts, :]
                                for b in range(nb)], axis=0).astype(BF16)

    for i in range(npiece):
        for k in range(i * ts, (i + 1) * ts):
            rows = pl.ds(k, 2 * nb, stride=pitch)
            for p in range(half):
                br = bu_ref[p, rows, :]
                bi = bu_ref[half + p, rows, :]
                hr[p], hi[p] = ar[p] * hr[p] - ai[p] * hi[p] + br, ar[p] * hi[p] + ai[p] * hr[p] + bi
                hh_ref[p, rows, :] = hr[p]
                hh_ref[half + p, rows, :] = hi[p]
        y = d_ref[0] * us[i].astype(F32)
        for p in range(half):
            both = lambda lb0: jnp.concatenate([states(lb0 + 2 * p, i), states(lb0 + 2 * p + 1, i)], axis=1)
            krows = slice(2 * p * LANES, (2 * p + 2) * LANES)
            y = y + jnp.dot(both(0), cr_ref[0, krows, :], preferred_element_type=F32)
            y = y - jnp.dot(both(nlb // 2), ci_ref[0, krows, :], preferred_element_type=F32)
        y_ref[:, i * ts:(i + 1) * ts, :] = y.reshape(nb, ts, LANES).astype(y_ref.dtype)
    for p in range(half):
        h_ref[p] = hr[p]
        h_ref[half + p] = hi[p]

    @pl.when(c == pl.num_programs(1) - 1)
    def _():
        for lb in range(nlb):
            p, q = divmod(lb, 2)
            hf_ref[0, :, lanes(lb)] = h_ref[p, q * nb:(q + 1) * nb, :]


def _ssm_scan(proj3, h0, bblk, crblk, ciblk, ar, ai, dsk):
    nb, l, _ = proj3.shape
    tk = next(t for t in (256, 128, l) if l % t == 0)
    sl = STATE_LANES
    pitch = tk + 4
    assert tk % 8 == 0
    npiece = 4 if tk % 64 == 0 else 1
    kern = functools.partial(_ssm_scan_kernel, nb=nb, tk=tk, pitch=pitch, npiece=npiece)
    return pl.pallas_call(
        kern,
        grid=(N_SSM_BLOCKS, l // tk),
        in_specs=[
            pl.BlockSpec((nb, tk, LANES), lambda s, c: (0, c, s)),
            pl.BlockSpec((1, LANES, 2 * sl), lambda s, c: (s, 0, 0)),
            pl.BlockSpec((1, sl, LANES), lambda s, c: (s, 0, 0)),
            pl.BlockSpec((1, sl, LANES), lambda s, c: (s, 0, 0)),
            pl.BlockSpec((1, 1, sl), lambda s, c: (s, 0, 0)),
            pl.BlockSpec((1, 1, sl), lambda s, c: (s, 0, 0)),
            pl.BlockSpec((1, 1, LANES), lambda s, c: (s, 0, 0)),
            pl.BlockSpec((1, 1, 2 * sl), lambda s, c: (s, 0, 0)),
        ],
        out_specs=[
            pl.BlockSpec((nb, tk, LANES), lambda s, c: (0, c, s)),
            pl.BlockSpec((1, nb, 2 * sl), lambda s, c: (s, 0, 0)),
        ],
        out_shape=[
            jax.ShapeDtypeStruct((nb, l, SSM_WIDTH), BF16),
            jax.ShapeDtypeStruct((N_SSM_BLOCKS, nb, 2 * sl), F32),
        ],
        scratch_shapes=[
            pltpu.VMEM((sl // LANES, 2 * nb * pitch, LANES), F32),
            pltpu.VMEM((sl // LANES, 2 * nb * pitch, LANES), F32),
            pltpu.VMEM((sl // LANES, 2 * nb, LANES), F32),
        ],
        compiler_params=pltpu.CompilerParams(
            dimension_semantics=("parallel", "arbitrary"), vmem_limit_bytes=VMEM_LIMIT),
        name="ssm_scan",
    )(proj3, bblk, crblk, ciblk, ar, ai, dsk, h0)


KEY_TILE = LANES
ROW_CHUNK = 2 * KEY_TILE


MASKED = -1e30
NEGLIGIBLE_LOG_WEIGHT = -88.0


def _sb_prepare(jobs, uu):
    zs = [lax.dot_general(qm, k, (((1,), (1,)), ((), ())), preferred_element_type=F32) for qm, k, _ in jobs]
    lss, hls = [], []
    for z, (_, _, valid) in zip(zs, jobs):
        l1p = jnp.log(1.0 + jnp.exp(-jnp.abs(z)))
        ls = jnp.minimum(z, 0.0) - l1p
        lk = ls - z
        if valid is not None:
            lk = jnp.where(valid, lk, 0.0)
        lss.append(ls)
        hls.append(lk.astype(BF16))
    rs = [jnp.dot(hl, uu, preferred_element_type=F32) for hl in hls]
    outs = []
    for ls, r, (_, _, valid) in zip(lss, rs, jobs):
        tk = ls.shape[1]
        pre = ls + r[:, :tk]
        if valid is not None:
            pre = jnp.where(valid, pre, MASKED)
        outs.append((pre, r[:, tk:]))
    return outs


def _attn_kernel(q_ref, k_ref, v_ref, km_ref, vm_ref, uu_ref, o_ref, qm_ref, c_ref, acc_ref,
                 pred_ref, rsd_ref, prem_ref, rsm_ref, prei_ref, rsi_ref, *, nsub, has_prefix, kmin):
    qi = pl.program_id(2)
    tk = KEY_TILE
    lane = lax.broadcasted_iota(jnp.int32, (tk, LANES), 1)
    for s in range(nsub):
        qs = q_ref[0, s * tk:(s + 1) * tk, :] * 0.125
        qm_ref[s * ROW_CHUNK:s * ROW_CHUNK + tk, :] = jnp.where(lane < HEAD_DIM, qs, jnp.zeros_like(qs))
        qm_ref[s * ROW_CHUNK + tk:(s + 1) * ROW_CHUNK, :] = jnp.where(lane >= HEAD_DIM, qs, jnp.zeros_like(qs))
    c_ref[...] = jnp.zeros_like(c_ref)
    acc_ref[...] = jnp.zeros_like(acc_ref)
    uu = uu_ref[...]

    col = lax.broadcasted_iota(jnp.int32, (ROW_CHUNK, tk), 1)
    row = lax.broadcasted_iota(jnp.int32, (ROW_CHUNK, tk), 0) & (tk - 1)
    key_ok = (col >= kmin) if kmin > 0 else None
    causal = (col < row) if key_ok is None else (col < row) & key_ok

    rows = lambda s: slice(s * ROW_CHUNK, (s + 1) * ROW_CHUNK)
    key_tile = lambda t: pl.ds(pl.multiple_of(t * tk, tk), tk)

    def prepare(tiles):
        jobs, dests = [], []
        for t, pre_ref, rs_ref, first_sub, mask_of in tiles:
            k = km_ref[...] if t is None else k_ref[0, key_tile(t), :]
            for s in range(first_sub, nsub):
                jobs.append((qm_ref[rows(s), :], k, mask_of(s)))
                dests.append((pre_ref, rs_ref, s))
        for (pre, rs), (pre_ref, rs_ref, s) in zip(_sb_prepare(jobs, uu), dests):
            pre_ref[rows(s), :] = pre
            rs_ref[rows(s), :] = rs

    def apply(t, pre_ref, rs_ref, first_sub):
        v = vm_ref[...] if t is None else v_ref[0, key_tile(t), :]
        for s in range(first_sub, nsub):
            c = c_ref[rows(s), :]
            w = jnp.exp(pre_ref[rows(s), :] + c).astype(BF16)
            c_ref[rows(s), :] = c + rs_ref[rows(s), :]
            acc_ref[rows(s), :] += jnp.dot(w, v, preferred_element_type=F32)

    no_mask = lambda s: None
    meta_mask = lambda s: col >= (tk - N_META)
    diag_mask = lambda j: (lambda s: causal if s == j else key_ok)
    n_before = qi * nsub
    earlier = lambda i: n_before - 1 - i

    prepare([(n_before + j, pred_ref.at[j], rsd_ref.at[j], j, diag_mask(j)) for j in range(nsub - 1, -1, -1)]
            + [(jnp.maximum(earlier(0), 0), prei_ref, rsi_ref, 0, no_mask)])
    for j in range(nsub - 1, -1, -1):
        apply(n_before + j, pred_ref.at[j], rsd_ref.at[j], j)

    def more(carry):
        i, cmax = carry
        return (i < n_before) & (cmax > NEGLIGIBLE_LOG_WEIGHT)

    def one_tile(carry):
        i, _ = carry

        @pl.when(i > 0)
        def _():
            prepare([(earlier(i), prei_ref, rsi_ref, 0, no_mask)])

        apply(earlier(i), prei_ref, rsi_ref, 0)
        return i + 1, jnp.max(c_ref[...])

    _, cmax = lax.while_loop(more, one_tile, (jnp.int32(0), jnp.max(c_ref[...])))
    if has_prefix:
        @pl.when(cmax > NEGLIGIBLE_LOG_WEIGHT)
        def _():
            prepare([(None, prem_ref, rsm_ref, 0, meta_mask)])
            apply(None, prem_ref, rsm_ref, 0)

    for s in range(nsub):
        a = acc_ref[s * ROW_CHUNK:s * ROW_CHUNK + tk, :]
        b = acc_ref[s * ROW_CHUNK + tk:(s + 1) * ROW_CHUNK, :]
        o_ref[0, s * tk:(s + 1) * tk, :] = jnp.where(lane < HEAD_DIM, a, b).astype(o_ref.dtype)


def _attention(proj3, kmeta, vmeta, uu, *, has_prefix, kmin):
    nb, l, _ = proj3.shape
    nsub = min(4, l // KEY_TILE)
    tq = nsub * KEY_TILE
    m = nsub * ROW_CHUNK
    nhp = ATTN_WIDTH // LANES
    qoff, koff, voff = SSM_WIDTH // LANES, (SSM_WIDTH + ATTN_WIDTH) // LANES, (SSM_WIDTH + 2 * ATTN_WIDTH) // LANES
    kern = functools.partial(_attn_kernel, nsub=nsub, has_prefix=has_prefix, kmin=kmin)
    return pl.pallas_call(
        kern,
        grid=(nb, nhp, l // tq),
        in_specs=[
            pl.BlockSpec((1, tq, LANES), lambda b, h, i: (b, i, qoff + h)),
            pl.BlockSpec((1, l, LANES), lambda b, h, i: (b, 0, koff + h)),
            pl.BlockSpec((1, l, LANES), lambda b, h, i: (b, 0, voff + h)),
            pl.BlockSpec((KEY_TILE, LANES), lambda b, h, i: (0, h)),
            pl.BlockSpec((KEY_TILE, LANES), lambda b, h, i: (0, h)),
            pl.BlockSpec((KEY_TILE, 2 * KEY_TILE), lambda b, h, i: (0, 0)),
        ],
        out_specs=pl.BlockSpec((1, tq, LANES), lambda b, h, i: (b, i, h)),
        out_shape=jax.ShapeDtypeStruct((nb, l, ATTN_WIDTH), BF16),
        scratch_shapes=[
            pltpu.VMEM((m, LANES), BF16),
            pltpu.VMEM((m, LANES), F32),
            pltpu.VMEM((m, LANES), F32),
            pltpu.VMEM((nsub, m, LANES), F32),
            pltpu.VMEM((nsub, m, LANES), F32),
            pltpu.VMEM((m, LANES), F32),
            pltpu.VMEM((m, LANES), F32),
            pltpu.VMEM((m, LANES), F32),
            pltpu.VMEM((m, LANES), F32),
        ],
        compiler_params=pltpu.CompilerParams(
            dimension_semantics=("parallel", "parallel", "arbitrary"), vmem_limit_bytes=VMEM_LIMIT),
        name="sb_attention",
    )(proj3, proj3, proj3, kmeta, vmeta, uu)


def _mix_out_kernel(ys_ref, ya_ref, x_ref, wglu_ref, bglu_ref, gs_ref, ga_ref, wout_ref, gf_ref,
                    h1_ref, hn_ref):
    y = jax.nn.gelu(ys_ref[...].astype(F32))
    gate = jnp.dot(y.astype(BF16), wglu_ref[...], preferred_element_type=F32) + bglu_ref[...]
    y = y * _sigmoid(gate)
    n1 = _rms(y, gs_ref[...]).astype(BF16)
    n2 = _rms(ya_ref[...].astype(F32), ga_ref[...]).astype(BF16)
    h1 = (x_ref[...]
          + jnp.dot(n1, wout_ref[:SSM_WIDTH, :], preferred_element_type=F32)
          + jnp.dot(n2, wout_ref[SSM_WIDTH:, :], preferred_element_type=F32))
    h1_ref[...] = h1
    hn_ref[...] = _rms(h1, gf_ref[...]).astype(BF16)


def _mix_out(ys, ya, x, wglu, bglu, gs, ga, wout, gf):
    m, d = x.shape
    tm = min(512, m)
    row = lambda w: pl.BlockSpec((tm, w), lambda i: (i, 0))
    full = lambda a: pl.BlockSpec(a.shape, lambda i: (0, 0))
    return pl.pallas_call(
        _mix_out_kernel,
        grid=(m // tm,),
        in_specs=[row(SSM_WIDTH), row(ATTN_WIDTH), row(d), full(wglu), full(bglu), full(gs), full(ga),
                  full(wout), full(gf)],
        out_specs=[row(d), row(d)],
        out_shape=[jax.ShapeDtypeStruct((m, d), F32), jax.ShapeDtypeStruct((m, d), BF16)],
        compiler_params=pltpu.CompilerParams(
            dimension_semantics=("parallel",), vmem_limit_bytes=VMEM_LIMIT),
        name="mix_out",
    )(ys, ya, x, wglu, bglu, gs, ga, wout, gf)


def _ffn_kernel(hn_ref, halo_ref, h1_ref, wg_ref, wv_ref, cw_ref, cb_ref, wd_ref,
                gfin_ref, o_ref, xh_ref, upg_ref, upv_ref, *, tm):
    j = pl.program_id(1)

    @pl.when(j == 0)
    def _():
        xh_ref[:HALO, :] = halo_ref[0]
        xh_ref[HALO:, :] = hn_ref[...]
        o_ref[...] = jnp.zeros_like(o_ref)

    n = tm // FFN_PIECES
    for h in range(FFN_PIECES):
        lo = 0 if h == 0 else h * n + HALO
        xs = xh_ref[lo:(h + 1) * n + HALO, :]
        upg_ref[lo:(h + 1) * n + HALO, :] = jnp.dot(xs, wg_ref[...], preferred_element_type=F32)
        upv_ref[lo:(h + 1) * n + HALO, :] = jnp.dot(xs, wv_ref[...], preferred_element_type=F32)

    tf = wg_ref.shape[1]
    gate_cols = pl.ds(pl.multiple_of(j * tf, tf), tf)
    val_cols = pl.ds(pl.multiple_of((pl.num_programs(1) + j) * tf, tf), tf)

    def conv(up_ref, cols, r0):
        return (cw_ref[0:1, cols] * up_ref[pl.ds(r0 + HALO - 2, n), :]
                + cw_ref[1:2, cols] * up_ref[pl.ds(r0 + HALO - 1, n), :]
                + cw_ref[2:3, cols] * up_ref[pl.ds(r0 + HALO, n), :]
                + cb_ref[:, cols])

    for h in range(FFN_PIECES):
        gate = conv(upg_ref, gate_cols, h * n)
        val = conv(upv_ref, val_cols, h * n)
        a = (gate * _sigmoid(gate) * val).astype(BF16)
        o_ref[h * n:(h + 1) * n, :] += jnp.dot(a, wd_ref[...], preferred_element_type=F32)

    @pl.when(j == pl.num_programs(1) - 1)
    def _():
        o_ref[...] = _rms(h1_ref[...] + o_ref[...], gfin_ref[...])


FFN_ROWS = 512
FFN_PIECES = 2


def _ffn(hn, halo, h1, wup, cw, cb, wd, gfin):
    m, d = hn.shape
    dff = wd.shape[0]
    tm = FFN_ROWS
    tf = 512
    nf = dff // tf
    kern = functools.partial(_ffn_kernel, tm=tm)
    return pl.pallas_call(
        kern,
        grid=(m // tm, nf),
        in_specs=[
            pl.BlockSpec((tm, d), lambda i, j: (i, 0)),
            pl.BlockSpec((1, HALO, d), lambda i, j: (i, 0, 0)),
            pl.BlockSpec((tm, d), lambda i, j: (i, 0)),
            pl.BlockSpec((d, tf), lambda i, j: (0, j)),
            pl.BlockSpec((d, tf), lambda i, j: (0, nf + j)),
            pl.BlockSpec(cw.shape, lambda i, j: (0, 0)),
            pl.BlockSpec(cb.shape, lambda i, j: (0, 0)),
            pl.BlockSpec((tf, d), lambda i, j: (j, 0)),
            pl.BlockSpec((1, d), lambda i, j: (0, 0)),
        ],
        out_specs=pl.BlockSpec((tm, d), lambda i, j: (i, 0)),
        out_shape=jax.ShapeDtypeStruct((m, d), F32),
        scratch_shapes=[
            pltpu.VMEM((tm + HALO, d), BF16),
            pltpu.VMEM((tm + HALO, tf), F32),
            pltpu.VMEM((tm + HALO, tf), F32),
        ],
        compiler_params=pltpu.CompilerParams(
            dimension_semantics=("parallel", "arbitrary"), vmem_limit_bytes=VMEM_LIMIT),
        name="conv_ffn",
    )(hn, halo, h1, wup, wup, cw, cb, wd, gfin)


def kernel(x, meta_tokens, norm_mix_g, w_in, ssm_lambda_re, ssm_lambda_im, ssm_log_dt, ssm_b_re, ssm_b_im, ssm_c_re, ssm_c_im, ssm_d, w_glu, b_glu, g_ssm_out, g_attn_out, w_out, norm_ffn_g, w_up, conv_w, conv_b, w_down, norm_final_g):
    assert w_in.shape[0] == 1, "single-layer block"
    nb, seq, d = x.shape
    m = nb * seq
    ffn_tile = FFN_ROWS
    assert seq % ffn_tile == 0

    w_in_b = w_in[0].astype(BF16)
    w_glu_b = w_glu[0].astype(BF16)
    w_out_b = w_out[0].astype(BF16)
    w_up_b = w_up[0].astype(BF16)
    w_down_b = w_down[0].astype(BF16)

    ar, ai, bbr, bbi = _ssm_discretize(ssm_lambda_re[0], ssm_lambda_im[0], ssm_log_dt[0],
                                       ssm_b_re[0], ssm_b_im[0])
    gl = GROUPS_PER_BLOCK
    per_block = lambda a: a.reshape((N_SSM_BLOCKS, gl) + a.shape[1:])
    to_in = lambda bb: _block_diag(jnp.swapaxes(per_block(bb), 2, 3))
    bblk = jnp.concatenate([to_in(bbr), to_in(bbi)], axis=-1).astype(BF16)
    to_out = lambda cc: _block_diag(jnp.swapaxes(per_block(cc), 2, 3)).astype(BF16)
    crblk = to_out(ssm_c_re[0].astype(F32))
    ciblk = to_out(ssm_c_im[0].astype(F32))
    ar_b = ar.reshape(N_SSM_BLOCKS, 1, STATE_LANES)
    ai_b = ai.reshape(N_SSM_BLOCKS, 1, STATE_LANES)
    dsk = ssm_d[0].astype(F32).reshape(N_SSM_BLOCKS, 1, LANES)

    rr = lax.broadcasted_iota(jnp.int32, (KEY_TILE, 2 * KEY_TILE), 0)
    cc = lax.broadcasted_iota(jnp.int32, (KEY_TILE, 2 * KEY_TILE), 1)
    uu = ((cc >= KEY_TILE) | (rr > cc)).astype(BF16)

    g_mix = norm_mix_g

    proj_m = _norm_inproj(meta_tokens.astype(F32), g_mix, w_in_b)
    h0 = jnp.zeros((N_SSM_BLOCKS, 1, 2 * STATE_LANES), F32)
    ys_m, hfin_m = _ssm_scan(proj_m[None], h0, bblk, crblk, ciblk, ar_b, ai_b, dsk)
    proj_m_pad = jnp.pad(proj_m, ((LANES - N_META, 0), (0, 0)))
    kmeta = proj_m_pad[:, SSM_WIDTH + ATTN_WIDTH:SSM_WIDTH + 2 * ATTN_WIDTH]
    vmeta = proj_m_pad[:, SSM_WIDTH + 2 * ATTN_WIDTH:]
    ya_m = _attention(proj_m_pad[None], kmeta, vmeta, uu, has_prefix=False,
                      kmin=LANES - N_META)[0, LANES - N_META:]
    mix_w = (w_glu_b, b_glu, g_ssm_out, g_attn_out, w_out_b, norm_ffn_g)
    _, hn_m = _mix_out(ys_m[0], ya_m, meta_tokens.astype(F32), *mix_w)

    xf = x.reshape(m, d)
    proj = _norm_inproj(xf, g_mix, w_in_b).reshape(nb, seq, -1)
    ys, _ = _ssm_scan(proj, hfin_m, bblk, crblk, ciblk, ar_b, ai_b, dsk)
    ya = _attention(proj, kmeta, vmeta, uu, has_prefix=True, kmin=0)
    h1, hn = _mix_out(ys.reshape(m, -1), ya.reshape(m, -1), xf, *mix_w)

    tiles = hn.reshape(nb, seq // ffn_tile, ffn_tile, d)
    prev_tail = tiles[:, :-1, ffn_tile - HALO:, :]
    first = jnp.broadcast_to(hn_m[None, None], (nb, 1, HALO, d))
    halo = jnp.concatenate([first, prev_tail], axis=1).reshape(m // ffn_tile, HALO, d)

    out = _ffn(hn, halo, h1, w_up_b, conv_w[0], conv_b, w_down_b, norm_final_g.reshape(1, d))
    return out.reshape(nb, seq, d)
```

```python
import functools

import jax
import jax.numpy as jnp
from jax import lax
from jax.experimental import pallas as pl
from jax.experimental.pallas import tpu as pltpu

F32 = jnp.float32
BF16 = jnp.bfloat16

N_META = 16
SSM_WIDTH = 1024
ATTN_WIDTH = 1024
SSM_GROUP_CH = 16
SSM_GROUPS = 64
SSM_STATE = 64
HEAD_DIM = 64
CONV_W = 3
RMS_EPS = 1e-6

LANES = 128
GROUPS_PER_BLOCK = LANES // SSM_GROUP_CH
N_SSM_BLOCKS = SSM_GROUPS // GROUPS_PER_BLOCK
STATE_LANES = GROUPS_PER_BLOCK * SSM_STATE
HALO = 16
VMEM_LIMIT = 56 * 1024 * 1024


def _rms(x, g):
    ms = jnp.mean(x * x, axis=-1, keepdims=True)
    return x * lax.rsqrt(ms + RMS_EPS) * g


def _sigmoid(x):
    return 1.0 / (1.0 + jnp.exp(-x))


def _norm_inproj_kernel(x_ref, g_ref, w_ref, o_ref, *, pieces):
    n = x_ref.shape[0] // pieces
    for p in range(pieces):
        rows = slice(p * n, (p + 1) * n)
        xn = _rms(x_ref[rows, :], g_ref[...]).astype(BF16)
        o_ref[rows, :] = jnp.dot(xn, w_ref[...], preferred_element_type=F32).astype(o_ref.dtype)


def _norm_inproj(x, g, w):
    m, d = x.shape
    n = w.shape[1]
    tm = min(512, m)
    kern = functools.partial(_norm_inproj_kernel, pieces=4 if tm % 64 == 0 else 1)
    return pl.pallas_call(
        kern,
        grid=(m // tm,),
        in_specs=[
            pl.BlockSpec((tm, d), lambda i: (i, 0)),
            pl.BlockSpec((1, d), lambda i: (0, 0)),
            pl.BlockSpec((d, n), lambda i: (0, 0), pipeline_mode=pl.Buffered(1)),
        ],
        out_specs=pl.BlockSpec((tm, n), lambda i: (i, 0)),
        out_shape=jax.ShapeDtypeStruct((m, n), BF16),
        compiler_params=pltpu.CompilerParams(
            dimension_semantics=("parallel",), vmem_limit_bytes=VMEM_LIMIT),
        name="norm_inproj",
    )(x, g, w)


def _ssm_discretize_kernel(lr_ref, li_ref, ldt_ref, bre_ref, bim_ref,
                           ar_ref, ai_ref, bbr_ref, bbi_ref):
    lr = jnp.minimum(lr_ref[...], -1e-4)
    li = li_ref[...]
    delta = jnp.exp(ldt_ref[...])
    mag = jnp.exp(lr * delta)
    ar = mag * jnp.cos(li * delta)
    ai = mag * jnp.sin(li * delta)
    den = lr * lr + li * li
    nr = ar - 1.0
    ni = ai
    fr = (nr * lr + ni * li) / den
    fi = (ni * lr - nr * li) / den
    bre = bre_ref[...]
    bim = bim_ref[...]
    ar_ref[...] = ar
    ai_ref[...] = ai
    bbr_ref[...] = fr * bre - fi * bim
    bbi_ref[...] = fr * bim + fi * bre


def _ssm_discretize(lam_re, lam_im, log_dt, b_re, b_im):
    g, p, h = b_re.shape
    rep = lambda a: jnp.repeat(a, h, axis=1)
    shape = jax.ShapeDtypeStruct((g, p * h), F32)
    ar, ai, bbr, bbi = pl.pallas_call(
        _ssm_discretize_kernel,
        out_shape=(shape, shape, shape, shape),
        name="ssm_discretize",
    )(rep(lam_re), rep(lam_im), jnp.broadcast_to(log_dt[:, None], (g, p * h)),
      b_re.reshape(g, p * h), b_im.reshape(g, p * h))
    return ar[:, ::h], ai[:, ::h], bbr.reshape(g, p, h), bbi.reshape(g, p, h)


def _block_diag(blocks):
    nb, gl, r, c = blocks.shape
    eye = jnp.eye(gl, dtype=blocks.dtype)
    return (blocks[:, :, :, None, :] * eye[None, :, None, :, None]).reshape(nb, gl * r, gl * c)


def _ssm_scan_kernel(u_ref, bb_ref, cr_ref, ci_ref, ar_ref, ai_ref, d_ref, h0_ref,
                     y_ref, hf_ref, bu_ref, hh_ref, h_ref, *, nb, tk, pitch, npiece):
    c = pl.program_id(1)
    nlb = 2 * STATE_LANES // LANES
    half = nlb // 4
    lanes = lambda lb: slice(lb * LANES, (lb + 1) * LANES)
    tile = lambda ref, p: jnp.concatenate(
        [jnp.broadcast_to(ref[0, :, lanes(2 * p + q)], (nb, LANES)) for q in range(2)], axis=0)

    @pl.when(c == 0)
    def _():
        for p in range(2 * half):
            h_ref[p] = tile(h0_ref, p)

    ts = tk // npiece
    us = [u_ref[:, i * ts:(i + 1) * ts, :].reshape(nb * ts, LANES) for i in range(npiece)]
    for i in range(npiece):
        bu = jnp.dot(us[i], bb_ref[0], preferred_element_type=F32)
        for lb in range(nlb):
            p, q = divmod(lb, 2)
            for b in range(nb):
                r0 = (q * nb + b) * pitch + i * ts
                bu_ref[p, r0:r0 + ts, :] = bu[b * ts:(b + 1) * ts, lanes(lb)]

    ar = [tile(ar_ref, p) for p in range(half)]
    ai = [tile(ai_ref, p) for p in range(half)]
    hr = [h_ref[p] for p in range(half)]
    hi = [h_ref[half + p] for p in range(half)]

    def states(lb, i):
        p, q = divmod(lb, 2)
        return jnp.concatenate([hh_ref[p, (q * nb + b) * pitch + i * ts:(q * nb + b) * pitch + (i + 1) * ts, :]
                                for b in range(nb)], axis=0).astype(BF16)

    for i in range(npiece):
        for k in range(i * ts, (i + 1) * ts):
            rows = pl.ds(k, 2 * nb, stride=pitch)
            for p in range(half):
                br = bu_ref[p, rows, :]
                bi = bu_ref[half + p, rows, :]
                hr[p], hi[p] = ar[p] * hr[p] - ai[p] * hi[p] + br, ar[p] * hi[p] + ai[p] * hr[p] + bi
                hh_ref[p, rows, :] = hr[p]
                hh_ref[half + p, rows, :] = hi[p]
        y = d_ref[0] * us[i].astype(F32)
        for p in range(half):
            both = lambda lb0: jnp.concatenate([states(lb0 + 2 * p, i), states(lb0 + 2 * p + 1, i)], axis=1)
            krows = slice(2 * p * LANES, (2 * p + 2) * LANES)
            y = y + jnp.dot(both(0), cr_ref[0, krows, :], preferred_element_type=F32)
            y = y - jnp.dot(both(nlb // 2), ci_ref[0, krows, :], preferred_element_type=F32)
        y_ref[:, i * ts:(i + 1) * ts, :] = y.reshape(nb, ts, LANES).astype(y_ref.dtype)
    for p in range(half):
        h_ref[p] = hr[p]
        h_ref[half + p] = hi[p]

    @pl.when(c == pl.num_programs(1) - 1)
    def _():
        for lb in range(nlb):
            p, q = divmod(lb, 2)
            hf_ref[0, :, lanes(lb)] = h_ref[p, q * nb:(q + 1) * nb, :]


def _ssm_scan(proj3, h0, bblk, crblk, ciblk, ar, ai, dsk):
    nb, l, _ = proj3.shape
    tk = next(t for t in (256, 128, l) if l % t == 0)
    sl = STATE_LANES
    pitch = tk + 4
    assert tk % 8 == 0
    npiece = 4 if tk % 64 == 0 else 1
    kern = functools.partial(_ssm_scan_kernel, nb=nb, tk=tk, pitch=pitch, npiece=npiece)
    return pl.pallas_call(
        kern,
        grid=(N_SSM_BLOCKS, l // tk),
        in_specs=[
            pl.BlockSpec((nb, tk, LANES), lambda s, c: (0, c, s)),
            pl.BlockSpec((1, LANES, 2 * sl), lambda s, c: (s, 0, 0)),
            pl.BlockSpec((1, sl, LANES), lambda s, c: (s, 0, 0)),
            pl.BlockSpec((1, sl, LANES), lambda s, c: (s, 0, 0)),
            pl.BlockSpec((1, 1, sl), lambda s, c: (s, 0, 0)),
            pl.BlockSpec((1, 1, sl), lambda s, c: (s, 0, 0)),
            pl.BlockSpec((1, 1, LANES), lambda s, c: (s, 0, 0)),
            pl.BlockSpec((1, 1, 2 * sl), lambda s, c: (s, 0, 0)),
        ],
        out_specs=[
            pl.BlockSpec((nb, tk, LANES), lambda s, c: (0, c, s)),
            pl.BlockSpec((1, nb, 2 * sl), lambda s, c: (s, 0, 0)),
        ],
        out_shape=[
            jax.ShapeDtypeStruct((nb, l, SSM_WIDTH), BF16),
            jax.ShapeDtypeStruct((N_SSM_BLOCKS, nb, 2 * sl), F32),
        ],
        scratch_shapes=[
            pltpu.VMEM((sl // LANES, 2 * nb * pitch, LANES), F32),
            pltpu.VMEM((sl // LANES, 2 * nb * pitch, LANES), F32),
            pltpu.VMEM((sl // LANES, 2 * nb, LANES), F32),
        ],
        compiler_params=pltpu.CompilerParams(
            dimension_semantics=("parallel", "arbitrary"), vmem_limit_bytes=VMEM_LIMIT),
        name="ssm_scan",
    )(proj3, bblk, crblk, ciblk, ar, ai, dsk, h0)


KEY_TILE = LANES
ROW_CHUNK = 2 * KEY_TILE


MASKED = -1e30
NEGLIGIBLE_LOG_WEIGHT = -88.0


def _sb_prepare(jobs, uu):
    zs = [lax.dot_general(qm, k, (((1,), (1,)), ((), ())), preferred_element_type=F32) for qm, k, _ in jobs]
    lss, hls = [], []
    for z, (_, _, valid) in zip(zs, jobs):
        l1p = jnp.log(1.0 + jnp.exp(-jnp.abs(z)))
        ls = jnp.minimum(z, 0.0) - l1p
        lk = ls - z
        if valid is not None:
            lk = jnp.where(valid, lk, 0.0)
        lss.append(ls)
        hls.append(lk.astype(BF16))
    rs = [jnp.dot(hl, uu, preferred_element_type=F32) for hl in hls]
    outs = []
    for ls, r, (_, _, valid) in zip(lss, rs, jobs):
        tk = ls.shape[1]
        pre = ls + r[:, :tk]
        if valid is not None:
            pre = jnp.where(valid, pre, MASKED)
        outs.append((pre, r[:, tk:]))
    return outs


def _attn_kernel(q_ref, k_ref, v_ref, km_ref, vm_ref, uu_ref, o_ref, qm_ref, c_ref, acc_ref,
                 pred_ref, rsd_ref, prem_ref, rsm_ref, prei_ref, rsi_ref, *, nsub, has_prefix, kmin):
    qi = pl.program_id(2)
    tk = KEY_TILE
    lane = lax.broadcasted_iota(jnp.int32, (tk, LANES), 1)
    for s in range(nsub):
        qs = q_ref[0, s * tk:(s + 1) * tk, :] * 0.125
        qm_ref[s * ROW_CHUNK:s * ROW_CHUNK + tk, :] = jnp.where(lane < HEAD_DIM, qs, jnp.zeros_like(qs))
        qm_ref[s * ROW_CHUNK + tk:(s + 1) * ROW_CHUNK, :] = jnp.where(lane >= HEAD_DIM, qs, jnp.zeros_like(qs))
    c_ref[...] = jnp.zeros_like(c_ref)
    acc_ref[...] = jnp.zeros_like(acc_ref)
    uu = uu_ref[...]

    col = lax.broadcasted_iota(jnp.int32, (ROW_CHUNK, tk), 1)
    row = lax.broadcasted_iota(jnp.int32, (ROW_CHUNK, tk), 0) & (tk - 1)
    key_ok = (col >= kmin) if kmin > 0 else None
    causal = (col < row) if key_ok is None else (col < row) & key_ok

    rows = lambda s: slice(s * ROW_CHUNK, (s + 1) * ROW_CHUNK)
    key_tile = lambda t: pl.ds(pl.multiple_of(t * tk, tk), tk)

    def prepare(tiles):
        jobs, dests = [], []
        for t, pre_ref, rs_ref, first_sub, mask_of in tiles:
            k = km_ref[...] if t is None else k_ref[0, key_tile(t), :]
            for s in range(first_sub, nsub):
                jobs.append((qm_ref[rows(s), :], k, mask_of(s)))
                dests.append((pre_ref, rs_ref, s))
        for (pre, rs), (pre_ref, rs_ref, s) in zip(_sb_prepare(jobs, uu), dests):
            pre_ref[rows(s), :] = pre
            rs_ref[rows(s), :] = rs

    def apply(t, pre_ref, rs_ref, first_sub):
        v = vm_ref[...] if t is None else v_ref[0, key_tile(t), :]
        for s in range(first_sub, nsub):
            c = c_ref[rows(s), :]
            w = jnp.exp(pre_ref[rows(s), :] + c).astype(BF16)
            c_ref[rows(s), :] = c + rs_ref[rows(s), :]
            acc_ref[rows(s), :] += jnp.dot(w, v, preferred_element_type=F32)

    no_mask = lambda s: None
    meta_mask = lambda s: col >= (tk - N_META)
    diag_mask = lambda j: (lambda s: causal if s == j else key_ok)
    n_before = qi * nsub
    earlier = lambda i: n_before - 1 - i

    prepare([(n_before + j, pred_ref.at[j], rsd_ref.at[j], j, diag_mask(j)) for j in range(nsub - 1, -1, -1)]
            + [(jnp.maximum(earlier(0), 0), prei_ref, rsi_ref, 0, no_mask)])
    for j in range(nsub - 1, -1, -1):
        apply(n_before + j, pred_ref.at[j], rsd_ref.at[j], j)

    def more(carry):
        i, cmax = carry
        return (i < n_before) & (cmax > NEGLIGIBLE_LOG_WEIGHT)

    def one_tile(carry):
        i, _ = carry

        @pl.when(i > 0)
        def _():
            prepare([(earlier(i), prei_ref, rsi_ref, 0, no_mask)])

        apply(earlier(i), prei_ref, rsi_ref, 0)
        return i + 1, jnp.max(c_ref[...])

    _, cmax = lax.while_loop(more, one_tile, (jnp.int32(0), jnp.max(c_ref[...])))
    if has_prefix:
        @pl.when(cmax > NEGLIGIBLE_LOG_WEIGHT)
        def _():
            prepare([(None, prem_ref, rsm_ref, 0, meta_mask)])
            apply(None, prem_ref, rsm_ref, 0)

    for s in range(nsub):
        a = acc_ref[s * ROW_CHUNK:s * ROW_CHUNK + tk, :]
        b = acc_ref[s * ROW_CHUNK + tk:(s + 1) * ROW_CHUNK, :]
        o_ref[0, s * tk:(s + 1) * tk, :] = jnp.where(lane < HEAD_DIM, a, b).astype(o_ref.dtype)


def _attention(proj3, kmeta, vmeta, uu, *, has_prefix, kmin):
    nb, l, _ = proj3.shape
    nsub = min(4, l // KEY_TILE)
    tq = nsub * KEY_TILE
    m = nsub * ROW_CHUNK
    nhp = ATTN_WIDTH // LANES
    qoff, koff, voff = SSM_WIDTH // LANES, (SSM_WIDTH + ATTN_WIDTH) // LANES, (SSM_WIDTH + 2 * ATTN_WIDTH) // LANES
    kern = functools.partial(_attn_kernel, nsub=nsub, has_prefix=has_prefix, kmin=kmin)
    return pl.pallas_call(
        kern,
        grid=(nb, nhp, l // tq),
        in_specs=[
            pl.BlockSpec((1, tq, LANES), lambda b, h, i: (b, i, qoff + h)),
            pl.BlockSpec((1, l, LANES), lambda b, h, i: (b, 0, koff + h)),
            pl.BlockSpec((1, l, LANES), lambda b, h, i: (b, 0, voff + h)),
            pl.BlockSpec((KEY_TILE, LANES), lambda b, h, i: (0, h)),
            pl.BlockSpec((KEY_TILE, LANES), lambda b, h, i: (0, h)),
            pl.BlockSpec((KEY_TILE, 2 * KEY_TILE), lambda b, h, i: (0, 0)),
        ],
        out_specs=pl.BlockSpec((1, tq, LANES), lambda b, h, i: (b, i, h)),
        out_shape=jax.ShapeDtypeStruct((nb, l, ATTN_WIDTH), BF16),
        scratch_shapes=[
            pltpu.VMEM((m, LANES), BF16),
            pltpu.VMEM((m, LANES), F32),
            pltpu.VMEM((m, LANES), F32),
            pltpu.VMEM((nsub, m, LANES), F32),
            pltpu.VMEM((nsub, m, LANES), F32),
            pltpu.VMEM((m, LANES), F32),
            pltpu.VMEM((m, LANES), F32),
            pltpu.VMEM((m, LANES), F32),
            pltpu.VMEM((m, LANES), F32),
        ],
        compiler_params=pltpu.CompilerParams(
            dimension_semantics=("parallel", "parallel", "arbitrary"), vmem_limit_bytes=VMEM_LIMIT),
        name="sb_attention",
    )(proj3, proj3, proj3, kmeta, vmeta, uu)


def _mix_out_kernel(ys_ref, ya_ref, x_ref, wglu_ref, bglu_ref, gs_ref, ga_ref, wout_ref, gf_ref,
                    h1_ref, hn_ref):
    y = jax.nn.gelu(ys_ref[...].astype(F32))
    gate = jnp.dot(y.astype(BF16), wglu_ref[...], preferred_element_type=F32) + bglu_ref[...]
    y = y * _sigmoid(gate)
    n1 = _rms(y, gs_ref[...]).astype(BF16)
    n2 = _rms(ya_ref[...].astype(F32), ga_ref[...]).astype(BF16)
    h1 = (x_ref[...]
          + jnp.dot(n1, wout_ref[:SSM_WIDTH, :], preferred_element_type=F32)
          + jnp.dot(n2, wout_ref[SSM_WIDTH:, :], preferred_element_type=F32))
    h1_ref[...] = h1
    hn_ref[...] = _rms(h1, gf_ref[...]).astype(BF16)


def _mix_out(ys, ya, x, wglu, bglu, gs, ga, wout, gf):
    m, d = x.shape
    tm = min(512, m)
    row = lambda w: pl.BlockSpec((tm, w), lambda i: (i, 0))
    full = lambda a: pl.BlockSpec(a.shape, lambda i: (0, 0))
    return pl.pallas_call(
        _mix_out_kernel,
        grid=(m // tm,),
        in_specs=[row(SSM_WIDTH), row(ATTN_WIDTH), row(d), full(wglu), full(bglu), full(gs), full(ga),
                  full(wout), full(gf)],
        out_specs=[row(d), row(d)],
        out_shape=[jax.ShapeDtypeStruct((m, d), F32), jax.ShapeDtypeStruct((m, d), BF16)],
        compiler_params=pltpu.CompilerParams(
            dimension_semantics=("parallel",), vmem_limit_bytes=VMEM_LIMIT),
        name="mix_out",
    )(ys, ya, x, wglu, bglu, gs, ga, wout, gf)


def _ffn_kernel(hn_ref, halo_ref, h1_ref, wg_ref, wv_ref, cw_ref, cb_ref, wd_ref,
                gfin_ref, o_ref, xh_ref, upg_ref, upv_ref, *, tm):
    j = pl.program_id(1)

    @pl.when(j == 0)
    def _():
        xh_ref[:HALO, :] = halo_ref[0]
        xh_ref[HALO:, :] = hn_ref[...]
        o_ref[...] = jnp.zeros_like(o_ref)

    n = tm // FFN_PIECES
    for h in range(FFN_PIECES):
        lo = 0 if h == 0 else h * n + HALO
        xs = xh_ref[lo:(h + 1) * n + HALO, :]
        upg_ref[lo:(h + 1) * n + HALO, :] = jnp.dot(xs, wg_ref[...], preferred_element_type=F32)
        upv_ref[lo:(h + 1) * n + HALO, :] = jnp.dot(xs, wv_ref[...], preferred_element_type=F32)

    tf = wg_ref.shape[1]
    gate_cols = pl.ds(pl.multiple_of(j * tf, tf), tf)
    val_cols = pl.ds(pl.multiple_of((pl.num_programs(1) + j) * tf, tf), tf)

    def conv(up_ref, cols, r0):
        return (cw_ref[0:1, cols] * up_ref[pl.ds(r0 + HALO - 2, n), :]
                + cw_ref[1:2, cols] * up_ref[pl.ds(r0 + HALO - 1, n), :]
                + cw_ref[2:3, cols] * up_ref[pl.ds(r0 + HALO, n), :]
                + cb_ref[:, cols])

    for h in range(FFN_PIECES):
        gate = conv(upg_ref, gate_cols, h * n)
        val = conv(upv_ref, val_cols, h * n)
        a = (gate * _sigmoid(gate) * val).astype(BF16)
        o_ref[h * n:(h + 1) * n, :] += jnp.dot(a, wd_ref[...], preferred_element_type=F32)

    @pl.when(j == pl.num_programs(1) - 1)
    def _():
        o_ref[...] = _rms(h1_ref[...] + o_ref[...], gfin_ref[...])


FFN_ROWS = 1024
FFN_PIECES = 4


def _ffn(hn, halo, h1, wup, cw, cb, wd, gfin):
    m, d = hn.shape
    dff = wd.shape[0]
    tm = FFN_ROWS
    tf = 512
    nf = dff // tf
    kern = functools.partial(_ffn_kernel, tm=tm)
    return pl.pallas_call(
        kern,
        grid=(m // tm, nf),
        in_specs=[
            pl.BlockSpec((tm, d), lambda i, j: (i, 0)),
            pl.BlockSpec((1, HALO, d), lambda i, j: (i, 0, 0)),
            pl.BlockSpec((tm, d), lambda i, j: (i, 0), pipeline_mode=pl.Buffered(1)),
            pl.BlockSpec((d, tf), lambda i, j: (0, j)),
            pl.BlockSpec((d, tf), lambda i, j: (0, nf + j)),
            pl.BlockSpec(cw.shape, lambda i, j: (0, 0)),
            pl.BlockSpec(cb.shape, lambda i, j: (0, 0)),
            pl.BlockSpec((tf, d), lambda i, j: (j, 0)),
            pl.BlockSpec((1, d), lambda i, j: (0, 0)),
        ],
        out_specs=pl.BlockSpec((tm, d), lambda i, j: (i, 0), pipeline_mode=pl.Buffered(1)),
        out_shape=jax.ShapeDtypeStruct((m, d), F32),
        scratch_shapes=[
            pltpu.VMEM((tm + HALO, d), BF16),
            pltpu.VMEM((tm + HALO, tf), F32),
            pltpu.VMEM((tm + HALO, tf), F32),
        ],
        compiler_params=pltpu.CompilerParams(
            dimension_semantics=("parallel", "arbitrary"), vmem_limit_bytes=VMEM_LIMIT),
        name="conv_ffn",
    )(hn, halo, h1, wup, wup, cw, cb, wd, gfin)


def kernel(x, meta_tokens, norm_mix_g, w_in, ssm_lambda_re, ssm_lambda_im, ssm_log_dt, ssm_b_re, ssm_b_im, ssm_c_re, ssm_c_im, ssm_d, w_glu, b_glu, g_ssm_out, g_attn_out, w_out, norm_ffn_g, w_up, conv_w, conv_b, w_down, norm_final_g):
    assert w_in.shape[0] == 1, "single-layer block"
    nb, seq, d = x.shape
    m = nb * seq
    ffn_tile = FFN_ROWS
    assert seq % ffn_tile == 0

    w_in_b = w_in[0].astype(BF16)
    w_glu_b = w_glu[0].astype(BF16)
    w_out_b = w_out[0].astype(BF16)
    w_up_b = w_up[0].astype(BF16)
    w_down_b = w_down[0].astype(BF16)

    ar, ai, bbr, bbi = _ssm_discretize(ssm_lambda_re[0], ssm_lambda_im[0], ssm_log_dt[0],
                                       ssm_b_re[0], ssm_b_im[0])
    gl = GROUPS_PER_BLOCK
    per_block = lambda a: a.reshape((N_SSM_BLOCKS, gl) + a.shape[1:])
    to_in = lambda bb: _block_diag(jnp.swapaxes(per_block(bb), 2, 3))
    bblk = jnp.concatenate([to_in(bbr), to_in(bbi)], axis=-1).astype(BF16)
    to_out = lambda cc: _block_diag(jnp.swapaxes(per_block(cc), 2, 3)).astype(BF16)
    crblk = to_out(ssm_c_re[0].astype(F32))
    ciblk = to_out(ssm_c_im[0].astype(F32))
    ar_b = ar.reshape(N_SSM_BLOCKS, 1, STATE_LANES)
    ai_b = ai.reshape(N_SSM_BLOCKS, 1, STATE_LANES)
    dsk = ssm_d[0].astype(F32).reshape(N_SSM_BLOCKS, 1, LANES)

    rr = lax.broadcasted_iota(jnp.int32, (KEY_TILE, 2 * KEY_TILE), 0)
    cc = lax.broadcasted_iota(jnp.int32, (KEY_TILE, 2 * KEY_TILE), 1)
    uu = ((cc >= KEY_TILE) | (rr > cc)).astype(BF16)

    g_mix = norm_mix_g

    proj_m = _norm_inproj(meta_tokens.astype(F32), g_mix, w_in_b)
    h0 = jnp.zeros((N_SSM_BLOCKS, 1, 2 * STATE_LANES), F32)
    ys_m, hfin_m = _ssm_scan(proj_m[None], h0, bblk, crblk, ciblk, ar_b, ai_b, dsk)
    proj_m_pad = jnp.pad(proj_m, ((LANES - N_META, 0), (0, 0)))
    kmeta = proj_m_pad[:, SSM_WIDTH + ATTN_WIDTH:SSM_WIDTH + 2 * ATTN_WIDTH]
    vmeta = proj_m_pad[:, SSM_WIDTH + 2 * ATTN_WIDTH:]
    ya_m = _attention(proj_m_pad[None], kmeta, vmeta, uu, has_prefix=False,
                      kmin=LANES - N_META)[0, LANES - N_META:]
    mix_w = (w_glu_b, b_glu, g_ssm_out, g_attn_out, w_out_b, norm_ffn_g)
    _, hn_m = _mix_out(ys_m[0], ya_m, meta_tokens.astype(F32), *mix_w)

    xf = x.reshape(m, d)
    proj = _norm_inproj(xf, g_mix, w_in_b).reshape(nb, seq, -1)
    ys, _ = _ssm_scan(proj, hfin_m, bblk, crblk, ciblk, ar_b, ai_b, dsk)
    ya = _attention(proj, kmeta, vmeta, uu, has_prefix=True, kmin=0)
    h1, hn = _mix_out(ys.reshape(m, -1), ya.reshape(m, -1), xf, *mix_w)

    tiles = hn.reshape(nb, seq // ffn_tile, ffn_tile, d)
    prev_tail = tiles[:, :-1, ffn_tile - HALO:, :]
    first = jnp.broadcast_to(hn_m[None, None], (nb, 1, HALO, d))
    halo = jnp.concatenate([first, prev_tail], axis=1).reshape(m // ffn_tile, HALO, d)

    out = _ffn(hn, halo, h1, w_up_b, conv_w[0], conv_b, w_down_b, norm_final_g.reshape(1, d))
    return out.reshape(nb, seq, d)
```

```python
import functools

import jax
import jax.numpy as jnp
from jax import lax
from jax.experimental import pallas as pl
from jax.experimental.pallas import tpu as pltpu

F32 = jnp.float32
BF16 = jnp.bfloat16

N_META = 16
SSM_WIDTH = 1024
ATTN_WIDTH = 1024
SSM_GROUP_CH = 16
SSM_GROUPS = 64
SSM_STATE = 64
HEAD_DIM = 64
CONV_W = 3
RMS_EPS = 1e-6

LANES = 128
GROUPS_PER_BLOCK = LANES // SSM_GROUP_CH
N_SSM_BLOCKS = SSM_GROUPS // GROUPS_PER_BLOCK
STATE_LANES = GROUPS_PER_BLOCK * SSM_STATE
HALO = 16
VMEM_LIMIT = 56 * 1024 * 1024


def _rms(x, g):
    ms = jnp.mean(x * x, axis=-1, keepdims=True)
    return x * lax.rsqrt(ms + RMS_EPS) * g


def _sigmoid(x):
    return 1.0 / (1.0 + jnp.exp(-x))


def _norm_inproj_kernel(x_ref, g_ref, w_ref, o_ref, *, pieces):
    n = x_ref.shape[0] // pieces
    for p in range(pieces):
        rows = slice(p * n, (p + 1) * n)
        xn = _rms(x_ref[rows, :], g_ref[...]).astype(BF16)
        o_ref[rows, :] = jnp.dot(xn, w_ref[...], preferred_element_type=F32).astype(o_ref.dtype)


def _norm_inproj(x, g, w):
    m, d = x.shape
    n = w.shape[1]
    tm = min(512, m)
    kern = functools.partial(_norm_inproj_kernel, pieces=4 if tm % 64 == 0 else 1)
    return pl.pallas_call(
        kern,
        grid=(m // tm,),
        in_specs=[
            pl.BlockSpec((tm, d), lambda i: (i, 0)),
            pl.BlockSpec((1, d), lambda i: (0, 0)),
            pl.BlockSpec((d, n), lambda i: (0, 0), pipeline_mode=pl.Buffered(1)),
        ],
        out_specs=pl.BlockSpec((tm, n), lambda i: (i, 0)),
        out_shape=jax.ShapeDtypeStruct((m, n), BF16),
        compiler_params=pltpu.CompilerParams(
            dimension_semantics=("parallel",), vmem_limit_bytes=VMEM_LIMIT),
        name="norm_inproj",
    )(x, g, w)


def _ssm_discretize_kernel(lr_ref, li_ref, ldt_ref, bre_ref, bim_ref,
                           ar_ref, ai_ref, bbr_ref, bbi_ref):
    lr = jnp.minimum(lr_ref[...], -1e-4)
    li = li_ref[...]
    delta = jnp.exp(ldt_ref[...])
    mag = jnp.exp(lr * delta)
    ar = mag * jnp.cos(li * delta)
    ai = mag * jnp.sin(li * delta)
    den = lr * lr + li * li
    nr = ar - 1.0
    ni = ai
    fr = (nr * lr + ni * li) / den
    fi = (ni * lr - nr * li) / den
    bre = bre_ref[...]
    bim = bim_ref[...]
    ar_ref[...] = ar
    ai_ref[...] = ai
    bbr_ref[...] = fr * bre - fi * bim
    bbi_ref[...] = fr * bim + fi * bre


def _ssm_discretize(lam_re, lam_im, log_dt, b_re, b_im):
    g, p, h = b_re.shape
    rep = lambda a: jnp.repeat(a, h, axis=1)
    shape = jax.ShapeDtypeStruct((g, p * h), F32)
    ar, ai, bbr, bbi = pl.pallas_call(
        _ssm_discretize_kernel,
        out_shape=(shape, shape, shape, shape),
        name="ssm_discretize",
    )(rep(lam_re), rep(lam_im), jnp.broadcast_to(log_dt[:, None], (g, p * h)),
      b_re.reshape(g, p * h), b_im.reshape(g, p * h))
    return ar[:, ::h], ai[:, ::h], bbr.reshape(g, p, h), bbi.reshape(g, p, h)


def _block_diag(blocks):
    nb, gl, r, c = blocks.shape
    eye = jnp.eye(gl, dtype=blocks.dtype)
    return (blocks[:, :, :, None, :] * eye[None, :, None, :, None]).reshape(nb, gl * r, gl * c)


def _ssm_scan_kernel(u_ref, bb_ref, cr_ref, ci_ref, ar_ref, ai_ref, d_ref, h0_ref,
                     y_ref, hf_ref, bu_ref, hh_ref, h_ref, *, nb, tk, pitch, npiece):
    c = pl.program_id(1)
    nlb = 2 * STATE_LANES // LANES
    half = nlb // 4
    lanes = lambda lb: slice(lb * LANES, (lb + 1) * LANES)
    tile = lambda ref, p: jnp.concatenate(
        [jnp.broadcast_to(ref[0, :, lanes(2 * p + q)], (nb, LANES)) for q in range(2)], axis=0)

    @pl.when(c == 0)
    def _():
        for p in range(2 * half):
            h_ref[p] = tile(h0_ref, p)

    ts = tk // npiece
    us = [u_ref[:, i * ts:(i + 1) * ts, :].reshape(nb * ts, LANES) for i in range(npiece)]
    for i in range(npiece):
        bu = jnp.dot(us[i], bb_ref[0], preferred_element_type=F32)
        for lb in range(nlb):
            p, q = divmod(lb, 2)
            for b in range(nb):
                r0 = (q * nb + b) * pitch + i * ts
                bu_ref[p, r0:r0 + ts, :] = bu[b * ts:(b + 1) * ts, lanes(lb)]

    ar = [tile(ar_ref, p) for p in range(half)]
    ai = [tile(ai_ref, p) for p in range(half)]
    hr = [h_ref[p] for p in range(half)]
    hi = [h_ref[half + p] for p in range(half)]

    def states(lb, i):
        p, q = divmod(lb, 2)
        return jnp.concatenate([hh_ref[p, (q * nb + b) * pitch + i * ts:(q * nb + b) * pitch + (i + 1) * ts, :]
                                for b in range(nb)], axis=0).astype(BF16)

    for i in range(npiece):
        for k in range(i * ts, (i + 1) * ts):
            rows = pl.ds(k, 2 * nb, stride=pitch)
            for p in range(half):
                br = bu_ref[p, rows, :]
                bi = bu_ref[half + p, rows, :]
                hr[p], hi[p] = ar[p] * hr[p] - ai[p] * hi[p] + br, ar[p] * hi[p] + ai[p] * hr[p] + bi
                hh_ref[p, rows, :] = hr[p]
                hh_ref[half + p, rows, :] = hi[p]
        y = d_ref[0] * us[i].astype(F32)
        for p in range(half):
            both = lambda lb0: jnp.concatenate([states(lb0 + 2 * p, i), states(lb0 + 2 * p + 1, i)], axis=1)
            krows = slice(2 * p * LANES, (2 * p + 2) * LANES)
            y = y + jnp.dot(both(0), cr_ref[0, krows, :], preferred_element_type=F32)
            y = y - jnp.dot(both(nlb // 2), ci_ref[0, krows, :], preferred_element_type=F32)
        y_ref[:, i * ts:(i + 1) * ts, :] = y.reshape(nb, ts, LANES).astype(y_ref.dtype)
    for p in range(half):
        h_ref[p] = hr[p]
        h_ref[half + p] = hi[p]

    @pl.when(c == pl.num_programs(1) - 1)
    def _():
        for lb in range(nlb):
            p, q = divmod(lb, 2)
            hf_ref[0, :, lanes(lb)] = h_ref[p, q * nb:(q + 1) * nb, :]


def _ssm_scan(proj3, h0, bblk, crblk, ciblk, ar, ai, dsk):
    nb, l, _ = proj3.shape
    tk = next(t for t in (512, 256, 128, l) if l % t == 0)
    sl = STATE_LANES
    pitch = tk + 4
    assert tk % 8 == 0
    npiece = 4 if tk % 64 == 0 else 1
    kern = functools.partial(_ssm_scan_kernel, nb=nb, tk=tk, pitch=pitch, npiece=npiece)
    return pl.pallas_call(
        kern,
        grid=(N_SSM_BLOCKS, l // tk),
        in_specs=[
            pl.BlockSpec((nb, tk, LANES), lambda s, c: (0, c, s)),
            pl.BlockSpec((1, LANES, 2 * sl), lambda s, c: (s, 0, 0)),
            pl.BlockSpec((1, sl, LANES), lambda s, c: (s, 0, 0)),
            pl.BlockSpec((1, sl, LANES), lambda s, c: (s, 0, 0)),
            pl.BlockSpec((1, 1, sl), lambda s, c: (s, 0, 0)),
            pl.BlockSpec((1, 1, sl), lambda s, c: (s, 0, 0)),
            pl.BlockSpec((1, 1, LANES), lambda s, c: (s, 0, 0)),
            pl.BlockSpec((1, 1, 2 * sl), lambda s, c: (s, 0, 0)),
        ],
        out_specs=[
            pl.BlockSpec((nb, tk, LANES), lambda s, c: (0, c, s)),
            pl.BlockSpec((1, nb, 2 * sl), lambda s, c: (s, 0, 0)),
        ],
        out_shape=[
            jax.ShapeDtypeStruct((nb, l, SSM_WIDTH), BF16),
            jax.ShapeDtypeStruct((N_SSM_BLOCKS, nb, 2 * sl), F32),
        ],
        scratch_shapes=[
            pltpu.VMEM((sl // LANES, 2 * nb * pitch, LANES), F32),
            pltpu.VMEM((sl // LANES, 2 * nb * pitch, LANES), F32),
            pltpu.VMEM((sl // LANES, 2 * nb, LANES), F32),
        ],
        compiler_params=pltpu.CompilerParams(
            dimension_semantics=("parallel", "arbitrary"), vmem_limit_bytes=VMEM_LIMIT),
        name="ssm_scan",
    )(proj3, bblk, crblk, ciblk, ar, ai, dsk, h0)


KEY_TILE = LANES
ROW_CHUNK = 2 * KEY_TILE


MASKED = -1e30
NEGLIGIBLE_LOG_WEIGHT = -88.0


def _sb_prepare(jobs, uu):
    zs = [lax.dot_general(qm, k, (((1,), (1,)), ((), ())), preferred_element_type=F32) for qm, k, _ in jobs]
    lss, hls = [], []
    for z, (_, _, valid) in zip(zs, jobs):
        l1p = jnp.log(1.0 + jnp.exp(-jnp.abs(z)))
        ls = jnp.minimum(z, 0.0) - l1p
        lk = ls - z
        if valid is not None:
            lk = jnp.where(valid, lk, 0.0)
        lss.append(ls)
        hls.append(lk.astype(BF16))
    rs = [jnp.dot(hl, uu, preferred_element_type=F32) for hl in hls]
    outs = []
    for ls, r, (_, _, valid) in zip(lss, rs, jobs):
        tk = ls.shape[1]
        pre = ls + r[:, :tk]
        if valid is not None:
            pre = jnp.where(valid, pre, MASKED)
        outs.append((pre, r[:, tk:]))
    return outs


def _attn_kernel(q_ref, k_ref, v_ref, km_ref, vm_ref, uu_ref, o_ref, qm_ref, c_ref, acc_ref,
                 pred_ref, rsd_ref, prem_ref, rsm_ref, prei_ref, rsi_ref, *, nsub, has_prefix, kmin):
    qi = pl.program_id(2)
    tk = KEY_TILE
    lane = lax.broadcasted_iota(jnp.int32, (tk, LANES), 1)
    for s in range(nsub):
        qs = q_ref[0, s * tk:(s + 1) * tk, :] * 0.125
        qm_ref[s * ROW_CHUNK:s * ROW_CHUNK + tk, :] = jnp.where(lane < HEAD_DIM, qs, jnp.zeros_like(qs))
        qm_ref[s * ROW_CHUNK + tk:(s + 1) * ROW_CHUNK, :] = jnp.where(lane >= HEAD_DIM, qs, jnp.zeros_like(qs))
    c_ref[...] = jnp.zeros_like(c_ref)
    acc_ref[...] = jnp.zeros_like(acc_ref)
    uu = uu_ref[...]

    col = lax.broadcasted_iota(jnp.int32, (ROW_CHUNK, tk), 1)
    row = lax.broadcasted_iota(jnp.int32, (ROW_CHUNK, tk), 0) & (tk - 1)
    key_ok = (col >= kmin) if kmin > 0 else None
    causal = (col < row) if key_ok is None else (col < row) & key_ok

    rows = lambda s: slice(s * ROW_CHUNK, (s + 1) * ROW_CHUNK)
    key_tile = lambda t: pl.ds(pl.multiple_of(t * tk, tk), tk)

    def prepare(tiles):
        jobs, dests = [], []
        for t, pre_ref, rs_ref, first_sub, mask_of in tiles:
            k = km_ref[...] if t is None else k_ref[0, key_tile(t), :]
            for s in range(first_sub, nsub):
                jobs.append((qm_ref[rows(s), :], k, mask_of(s)))
                dests.append((pre_ref, rs_ref, s))
        for (pre, rs), (pre_ref, rs_ref, s) in zip(_sb_prepare(jobs, uu), dests):
            pre_ref[rows(s), :] = pre
            rs_ref[rows(s), :] = rs

    def apply(t, pre_ref, rs_ref, first_sub):
        v = vm_ref[...] if t is None else v_ref[0, key_tile(t), :]
        for s in range(first_sub, nsub):
            c = c_ref[rows(s), :]
            w = jnp.exp(pre_ref[rows(s), :] + c).astype(BF16)
            c_ref[rows(s), :] = c + rs_ref[rows(s), :]
            acc_ref[rows(s), :] += jnp.dot(w, v, preferred_element_type=F32)

    no_mask = lambda s: None
    meta_mask = lambda s: col >= (tk - N_META)
    diag_mask = lambda j: (lambda s: causal if s == j else key_ok)
    n_before = qi * nsub
    earlier = lambda i: n_before - 1 - i

    prepare([(n_before + j, pred_ref.at[j], rsd_ref.at[j], j, diag_mask(j)) for j in range(nsub - 1, -1, -1)]
            + [(jnp.maximum(earlier(0), 0), prei_ref, rsi_ref, 0, no_mask)])
    for j in range(nsub - 1, -1, -1):
        apply(n_before + j, pred_ref.at[j], rsd_ref.at[j], j)

    def more(carry):
        i, cmax = carry
        return (i < n_before) & (cmax > NEGLIGIBLE_LOG_WEIGHT)

    def one_tile(carry):
        i, _ = carry

        @pl.when(i > 0)
        def _():
            prepare([(earlier(i), prei_ref, rsi_ref, 0, no_mask)])

        apply(earlier(i), prei_ref, rsi_ref, 0)
        return i + 1, jnp.max(c_ref[...])

    _, cmax = lax.while_loop(more, one_tile, (jnp.int32(0), jnp.max(c_ref[...])))
    if has_prefix:
        @pl.when(cmax > NEGLIGIBLE_LOG_WEIGHT)
        def _():
            prepare([(None, prem_ref, rsm_ref, 0, meta_mask)])
            apply(None, prem_ref, rsm_ref, 0)

    for s in range(nsub):
        a = acc_ref[s * ROW_CHUNK:s * ROW_CHUNK + tk, :]
        b = acc_ref[s * ROW_CHUNK + tk:(s + 1) * ROW_CHUNK, :]
        o_ref[0, s * tk:(s + 1) * tk, :] = jnp.where(lane < HEAD_DIM, a, b).astype(o_ref.dtype)


def _attention(proj3, kmeta, vmeta, uu, *, has_prefix, kmin):
    nb, l, _ = proj3.shape
    nsub = min(4, l // KEY_TILE)
    tq = nsub * KEY_TILE
    m = nsub * ROW_CHUNK
    nhp = ATTN_WIDTH // LANES
    qoff, koff, voff = SSM_WIDTH // LANES, (SSM_WIDTH + ATTN_WIDTH) // LANES, (SSM_WIDTH + 2 * ATTN_WIDTH) // LANES
    kern = functools.partial(_attn_kernel, nsub=nsub, has_prefix=has_prefix, kmin=kmin)
    return pl.pallas_call(
        kern,
        grid=(nb, nhp, l // tq),
        in_specs=[
            pl.BlockSpec((1, tq, LANES), lambda b, h, i: (b, i, qoff + h)),
            pl.BlockSpec((1, l, LANES), lambda b, h, i: (b, 0, koff + h)),
            pl.BlockSpec((1, l, LANES), lambda b, h, i: (b, 0, voff + h)),
            pl.BlockSpec((KEY_TILE, LANES), lambda b, h, i: (0, h)),
            pl.BlockSpec((KEY_TILE, LANES), lambda b, h, i: (0, h)),
            pl.BlockSpec((KEY_TILE, 2 * KEY_TILE), lambda b, h, i: (0, 0)),
        ],
        out_specs=pl.BlockSpec((1, tq, LANES), lambda b, h, i: (b, i, h)),
        out_shape=jax.ShapeDtypeStruct((nb, l, ATTN_WIDTH), BF16),
        scratch_shapes=[
            pltpu.VMEM((m, LANES), BF16),
            pltpu.VMEM((m, LANES), F32),
            pltpu.VMEM((m, LANES), F32),
            pltpu.VMEM((nsub, m, LANES), F32),
            pltpu.VMEM((nsub, m, LANES), F32),
            pltpu.VMEM((m, LANES), F32),
            pltpu.VMEM((m, LANES), F32),
            pltpu.VMEM((m, LANES), F32),
            pltpu.VMEM((m, LANES), F32),
        ],
        compiler_params=pltpu.CompilerParams(
            dimension_semantics=("parallel", "parallel", "arbitrary"), vmem_limit_bytes=VMEM_LIMIT),
        name="sb_attention",
    )(proj3, proj3, proj3, kmeta, vmeta, uu)


def _mix_out_kernel(ys_ref, ya_ref, x_ref, wglu_ref, bglu_ref, gs_ref, ga_ref, wout_ref, gf_ref,
                    h1_ref, hn_ref):
    y = jax.nn.gelu(ys_ref[...].astype(F32))
    gate = jnp.dot(y.astype(BF16), wglu_ref[...], preferred_element_type=F32) + bglu_ref[...]
    y = y * _sigmoid(gate)
    n1 = _rms(y, gs_ref[...]).astype(BF16)
    n2 = _rms(ya_ref[...].astype(F32), ga_ref[...]).astype(BF16)
    h1 = (x_ref[...]
          + jnp.dot(n1, wout_ref[:SSM_WIDTH, :], preferred_element_type=F32)
          + jnp.dot(n2, wout_ref[SSM_WIDTH:, :], preferred_element_type=F32))
    h1_ref[...] = h1
    hn_ref[...] = _rms(h1, gf_ref[...]).astype(BF16)


def _mix_out(ys, ya, x, wglu, bglu, gs, ga, wout, gf):
    m, d = x.shape
    tm = min(512, m)
    row = lambda w: pl.BlockSpec((tm, w), lambda i: (i, 0))
    full = lambda a: pl.BlockSpec(a.shape, lambda i: (0, 0))
    return pl.pallas_call(
        _mix_out_kernel,
        grid=(m // tm,),
        in_specs=[row(SSM_WIDTH), row(ATTN_WIDTH), row(d), full(wglu), full(bglu), full(gs), full(ga),
                  full(wout), full(gf)],
        out_specs=[row(d), row(d)],
        out_shape=[jax.ShapeDtypeStruct((m, d), F32), jax.ShapeDtypeStruct((m, d), BF16)],
        compiler_params=pltpu.CompilerParams(
            dimension_semantics=("parallel",), vmem_limit_bytes=VMEM_LIMIT),
        name="mix_out",
    )(ys, ya, x, wglu, bglu, gs, ga, wout, gf)


def _ffn_kernel(hn_ref, halo_ref, h1_ref, wg_ref, wv_ref, cw_ref, cb_ref, wd_ref,
                gfin_ref, o_ref, xh_ref, upg_ref, upv_ref, *, tm):
    j = pl.program_id(1)

    @pl.when(j == 0)
    def _():
        xh_ref[:HALO, :] = halo_ref[0]
        xh_ref[HALO:, :] = hn_ref[...]
        o_ref[...] = jnp.zeros_like(o_ref)

    n = tm // FFN_PIECES
    for h in range(FFN_PIECES):
        lo = 0 if h == 0 else h * n + HALO
        xs = xh_ref[lo:(h + 1) * n + HALO, :]
        upg_ref[lo:(h + 1) * n + HALO, :] = jnp.dot(xs, wg_ref[...], preferred_element_type=F32)
        upv_ref[lo:(h + 1) * n + HALO, :] = jnp.dot(xs, wv_ref[...], preferred_element_type=F32)

    tf = wg_ref.shape[1]
    gate_cols = pl.ds(pl.multiple_of(j * tf, tf), tf)
    val_cols = pl.ds(pl.multiple_of((pl.num_programs(1) + j) * tf, tf), tf)

    def conv(up_ref, cols, r0):
        return (cw_ref[0:1, cols] * up_ref[pl.ds(r0 + HALO - 2, n), :]
                + cw_ref[1:2, cols] * up_ref[pl.ds(r0 + HALO - 1, n), :]
                + cw_ref[2:3, cols] * up_ref[pl.ds(r0 + HALO, n), :]
                + cb_ref[:, cols])

    for h in range(FFN_PIECES):
        gate = conv(upg_ref, gate_cols, h * n)
        val = conv(upv_ref, val_cols, h * n)
        a = (gate * _sigmoid(gate) * val).astype(BF16)
        o_ref[h * n:(h + 1) * n, :] += jnp.dot(a, wd_ref[...], preferred_element_type=F32)

    @pl.when(j == pl.num_programs(1) - 1)
    def _():
        o_ref[...] = _rms(h1_ref[...] + o_ref[...], gfin_ref[...])


FFN_ROWS = 512
FFN_PIECES = 2


def _ffn(hn, halo, h1, wup, cw, cb, wd, gfin):
    m, d = hn.shape
    dff = wd.shape[0]
    tm = FFN_ROWS
    tf = 512
    nf = dff // tf
    kern = functools.partial(_ffn_kernel, tm=tm)
    return pl.pallas_call(
        kern,
        grid=(m // tm, nf),
        in_specs=[
            pl.BlockSpec((tm, d), lambda i, j: (i, 0)),
            pl.BlockSpec((1, HALO, d), lambda i, j: (i, 0, 0)),
            pl.BlockSpec((tm, d), lambda i, j: (i, 0)),
            pl.BlockSpec((d, tf), lambda i, j: (0, j)),
            pl.BlockSpec((d, tf), lambda i, j: (0, nf + j)),
            pl.BlockSpec(cw.shape, lambda i, j: (0, 0)),
            pl.BlockSpec(cb.shape, lambda i, j: (0, 0)),
            pl.BlockSpec((tf, d), lambda i, j: (j, 0)),
            pl.BlockSpec((1, d), lambda i, j: (0, 0)),
        ],
        out_specs=pl.BlockSpec((tm, d), lambda i, j: (i, 0)),
        out_shape=jax.ShapeDtypeStruct((m, d), F32),
        scratch_shapes=[
            pltpu.VMEM((tm + HALO, d), BF16),
            pltpu.VMEM((tm + HALO, tf), F32),
            pltpu.VMEM((tm + HALO, tf), F32),
        ],
        compiler_params=pltpu.CompilerParams(
            dimension_semantics=("parallel", "arbitrary"), vmem_limit_bytes=VMEM_LIMIT),
        name="conv_ffn",
    )(hn, halo, h1, wup, wup, cw, cb, wd, gfin)


def kernel(x, meta_tokens, norm_mix_g, w_in, ssm_lambda_re, ssm_lambda_im, ssm_log_dt, ssm_b_re, ssm_b_im, ssm_c_re, ssm_c_im, ssm_d, w_glu, b_glu, g_ssm_out, g_attn_out, w_out, norm_ffn_g, w_up, conv_w, conv_b, w_down, norm_final_g):
    assert w_in.shape[0] == 1, "single-layer block"
    nb, seq, d = x.shape
    m = nb * seq
    ffn_tile = FFN_ROWS
    assert seq % ffn_tile == 0

    w_in_b = w_in[0].astype(BF16)
    w_glu_b = w_glu[0].astype(BF16)
    w_out_b = w_out[0].astype(BF16)
    w_up_b = w_up[0].astype(BF16)
    w_down_b = w_down[0].astype(BF16)

    ar, ai, bbr, bbi = _ssm_discretize(ssm_lambda_re[0], ssm_lambda_im[0], ssm_log_dt[0],
                                       ssm_b_re[0], ssm_b_im[0])
    gl = GROUPS_PER_BLOCK
    per_block = lambda a: a.reshape((N_SSM_BLOCKS, gl) + a.shape[1:])
    to_in = lambda bb: _block_diag(jnp.swapaxes(per_block(bb), 2, 3))
    bblk = jnp.concatenate([to_in(bbr), to_in(bbi)], axis=-1).astype(BF16)
    to_out = lambda cc: _block_diag(jnp.swapaxes(per_block(cc), 2, 3)).astype(BF16)
    crblk = to_out(ssm_c_re[0].astype(F32))
    ciblk = to_out(ssm_c_im[0].astype(F32))
    ar_b = ar.reshape(N_SSM_BLOCKS, 1, STATE_LANES)
    ai_b = ai.reshape(N_SSM_BLOCKS, 1, STATE_LANES)
    dsk = ssm_d[0].astype(F32).reshape(N_SSM_BLOCKS, 1, LANES)

    rr = lax.broadcasted_iota(jnp.int32, (KEY_TILE, 2 * KEY_TILE), 0)
    cc = lax.broadcasted_iota(jnp.int32, (KEY_TILE, 2 * KEY_TILE), 1)
    uu = ((cc >= KEY_TILE) | (rr > cc)).astype(BF16)

    g_mix = norm_mix_g

    proj_m = _norm_inproj(meta_tokens.astype(F32), g_mix, w_in_b)
    h0 = jnp.zeros((N_SSM_BLOCKS, 1, 2 * STATE_LANES), F32)
    ys_m, hfin_m = _ssm_scan(proj_m[None], h0, bblk, crblk, ciblk, ar_b, ai_b, dsk)
    proj_m_pad = jnp.pad(proj_m, ((LANES - N_META, 0), (0, 0)))
    kmeta = proj_m_pad[:, SSM_WIDTH + ATTN_WIDTH:SSM_WIDTH + 2 * ATTN_WIDTH]
    vmeta = proj_m_pad[:, SSM_WIDTH + 2 * ATTN_WIDTH:]
    ya_m = _attention(proj_m_pad[None], kmeta, vmeta, uu, has_prefix=False,
                      kmin=LANES - N_META)[0, LANES - N_META:]
    mix_w = (w_glu_b, b_glu, g_ssm_out, g_attn_out, w_out_b, norm_ffn_g)
    _, hn_m = _mix_out(ys_m[0], ya_m, meta_tokens.astype(F32), *mix_w)

    xf = x.reshape(m, d)
    proj = _norm_inproj(xf, g_mix, w_in_b).reshape(nb, seq, -1)
    ys, _ = _ssm_scan(proj, hfin_m, bblk, crblk, ciblk, ar_b, ai_b, dsk)
    ya = _attention(proj, kmeta, vmeta, uu, has_prefix=True, kmin=0)
    h1, hn = _mix_out(ys.reshape(m, -1), ya.reshape(m, -1), xf, *mix_w)

    tiles = hn.reshape(nb, seq // ffn_tile, ffn_tile, d)
    prev_tail = tiles[:, :-1, ffn_tile - HALO:, :]
    first = jnp.broadcast_to(hn_m[None, None], (nb, 1, HALO, d))
    halo = jnp.concatenate([first, prev_tail], axis=1).reshape(m // ffn_tile, HALO, d)

    out = _ffn(hn, halo, h1, w_up_b, conv_w[0], conv_b, w_down_b, norm_final_g.reshape(1, d))
    return out.reshape(nb, seq, d)
```
